```python
import math
import jax, jax.numpy as jnp
from jax import lax
import numpy as np

D_MODEL = 2048
BATCH = 32
SEQ = 256
DEPTH = 4
DEC_BATCH = 2
DEC_SEQ = 4096
PAST_LEN = 256

GRID_W = 64
HEAD_DIM = 128
MIX_W = D_MODEL
ATT_W = MIX_W // 2
ML_W = MIX_W // 4
DF_W = MIX_W // 4
ATT_HEADS = ATT_W // HEAD_DIM
ATT_KV_HEADS = 2
KV_W = ATT_KV_HEADS * HEAD_DIM
MLSTM_HEADS = ML_W // HEAD_DIM
DIFF_HEADS = DF_W // HEAD_DIM
DIFF_QK_DIM = HEAD_DIM // 2
N_DIR = 2
N_GATES = N_DIR * 2 * MLSTM_HEADS
D_FF = 4 * D_MODEL
QBLOCK = 128
MLSTM_CHUNK = 128
ROPE_THETA = 10000.0
EPS = 1e-6
SPLITS = (ATT_W, KV_W, KV_W, ML_W, ML_W, ML_W, ML_W, N_GATES, DF_W, DF_W, DF_W)
IN_W = sum(SPLITS)

kernel_name = 'hybrid_gqa_mlstm_diffattn_dit_step'


def rms_norm(x, g):
    xf = x.astype(jnp.float32)
    y = xf * lax.rsqrt(jnp.mean(xf * xf, axis=-1, keepdims=True) + EPS)
    return (y * g.astype(jnp.float32)).astype(x.dtype)


def axial_rope(n_tok, dim):
    rows = n_tok // GRID_W
    row_idx = jnp.repeat(jnp.arange(rows), GRID_W).astype(jnp.float32)
    col_idx = jnp.tile(jnp.arange(GRID_W), rows).astype(jnp.float32)
    n_freq = dim // 4
    inv = ROPE_THETA ** (-jnp.arange(n_freq, dtype=jnp.float32) / n_freq)
    ang = jnp.concatenate([row_idx[:, None] * inv, col_idx[:, None] * inv], axis=-1)
    return jnp.cos(ang), jnp.sin(ang)


def apply_rope(x, cos, sin):
    xf = x.astype(jnp.float32)
    half = x.shape[-1] // 2
    x1, x2 = xf[..., :half], xf[..., half:]
    c, s = cos[None, :, None, :], sin[None, :, None, :]
    return jnp.concatenate([x1 * c - x2 * s, x1 * s + x2 * c], axis=-1).astype(x.dtype)


def rope_two_maps(x, cos, sin):
    return jnp.concatenate([apply_rope(x[..., :DIFF_QK_DIM], cos, sin),
                            apply_rope(x[..., DIFF_QK_DIM:], cos, sin)], axis=-1)


def sweep_query_blocks(fn, *qs):
    b, n = qs[0].shape[:2]
    nb = n // QBLOCK
    blocks = tuple(jnp.moveaxis(q.reshape(b, nb, QBLOCK, *q.shape[2:]), 1, 0) for q in qs)
    out = lax.map(lambda blk: fn(*blk), blocks)
    out = jnp.moveaxis(out, 0, 1)
    return out.reshape(b, n, *out.shape[3:])


def gqa_attention(q, k, v):
    g = k.shape[2]
    scale = q.shape[-1] ** -0.5

    def block(qb):
        b, nq, h, d = qb.shape
        qg = qb.reshape(b, nq, g, h // g, d)
        s = jnp.einsum('bqgrd,btgd->bgrqt', qg, k).astype(jnp.float32) * scale
        p = jax.nn.softmax(s, axis=-1).astype(v.dtype)
        o = jnp.einsum('bgrqt,btgd->bqgrd', p, v)
        return o.reshape(b, nq, h, d)

    return sweep_query_blocks(block, q)


def diff_attention(q1, q2, k1, k2, v, lam):
    scale = q1.shape[-1] ** -0.5

    def block(qb1, qb2):
        s1 = jnp.einsum('bqhd,bthd->bhqt', qb1, k1).astype(jnp.float32) * scale
        s2 = jnp.einsum('bqhd,bthd->bhqt', qb2, k2).astype(jnp.float32) * scale
        p = jax.nn.softmax(s1, axis=-1) - lam * jax.nn.softmax(s2, axis=-1)
        return jnp.einsum('bhqt,bthe->bqhe', p.astype(v.dtype), v)

    return sweep_query_blocks(block, q1, q2)


def mlstm_scan(q, k, v, ig, lf, state):
    b, s, h, d = q.shape
    nc = s // MLSTM_CHUNK
    causal = jnp.tril(jnp.ones((MLSTM_CHUNK, MLSTM_CHUNK), dtype=bool))

    def to_chunks(a):
        return jnp.moveaxis(a.reshape(b, nc, MLSTM_CHUNK, *a.shape[2:]), 1, 0)

    def step(carry, xs):
        C, n, m = carry
        qc, kc, vc, ic, fc = xs
        bcum = jnp.cumsum(fc, axis=1)
        logw = bcum[:, :, None, :] - bcum[:, None, :, :] + ic[:, None, :, :]
        logw = jnp.where(causal[None, :, :, None], logw, -jnp.inf)
        inter = bcum + m[:, None, :]
        m_row = jnp.maximum(inter, jnp.max(logw, axis=2))
        sc = jnp.einsum('bjhd,bshd->bjsh', qc, kc) * jnp.exp(logw - m_row[:, :, None, :])
        a_inter = jnp.exp(inter - m_row)
        num = (jnp.einsum('bjsh,bshe->bjhe', sc, vc)
               + a_inter[..., None] * jnp.einsum('bjhd,bhde->bjhe', qc, C))
        den = jnp.sum(sc, axis=2) + a_inter * jnp.einsum('bjhd,bhd->bjh', qc, n)
        hc = num / jnp.maximum(jnp.abs(den), jnp.exp(-m_row))[..., None]
        m_new = m_row[:, -1]
        w_end = jnp.exp(bcum[:, -1:, :] - bcum + ic - m_new[:, None, :])
        decay = jnp.exp(bcum[:, -1] + m - m_new)
        C_new = decay[..., None, None] * C + jnp.einsum('bsh,bshd,bshe->bhde', w_end, kc, vc)
        n_new = decay[..., None] * n + jnp.einsum('bsh,bshd->bhd', w_end, kc)
        return (C_new, n_new, m_new), hc

    state, hs = lax.scan(step, state, tuple(to_chunks(a) for a in (q, k, v, ig, lf)))
    hs = jnp.moveaxis(hs, 0, 1).reshape(b, s, h, d)
    return hs, state


def mlstm_bidir(q, k, v, gates, st_f, st_b):
    flip = lambda a: jnp.flip(a, axis=1)
    ig = gates[:, :, :, 0]
    lf = jax.nn.log_sigmoid(gates[:, :, :, 1])
    h_f, st_f = mlstm_scan(q, k, v, ig[:, :, 0], lf[:, :, 0], st_f)
    h_b, st_b = mlstm_scan(flip(q), flip(k), flip(v), flip(ig[:, :, 1]), flip(lf[:, :, 1]), st_b)
    return h_f + flip(h_b), st_f, st_b


def mixing(hn, lp, layer, ctx):
    f32 = jnp.float32
    b, n, _ = hn.shape
    proj = hn @ lp['w_in']
    idx = np.cumsum(SPLITS)[:-1].tolist()
    aq, ak, av, mq, mk, mv, mo, mg, dq, dk, dv = jnp.split(proj, idx, axis=-1)
    heads = lambda a, h: a.reshape(b, n, h, HEAD_DIM)
    aq = rms_norm(heads(aq, ATT_HEADS), lp['qk_gain'][0])
    ak = rms_norm(heads(ak, ATT_KV_HEADS), lp['qk_gain'][1])
    av = heads(av, ATT_KV_HEADS)
    dq, dk, dv = heads(dq, DIFF_HEADS), heads(dk, DIFF_HEADS), heads(dv, DIFF_HEADS)
    mq = heads(mq, MLSTM_HEADS).astype(f32)
    mk = heads(mk, MLSTM_HEADS).astype(f32) * HEAD_DIM ** -0.5
    mv = heads(mv, MLSTM_HEADS).astype(f32)
    gates = mg.astype(f32).reshape(b, n, N_DIR, 2, MLSTM_HEADS) + lp['gate_bias'].astype(f32)

    if ctx is None:
        q_att, k_att, v_att = aq, ak, av
        q_dif, k_dif, v_dif = dq, dk, dv
        zero = (jnp.zeros((b, MLSTM_HEADS, HEAD_DIM, HEAD_DIM), f32),
                jnp.zeros((b, MLSTM_HEADS, HEAD_DIM), f32),
                jnp.full((b, MLSTM_HEADS), -jnp.inf, f32))
        st_f, st_b = zero, zero
    else:
        cos, sin = axial_rope(n, HEAD_DIM)
        cos2, sin2 = axial_rope(n, DIFF_QK_DIM)
        q_att = apply_rope(aq, cos, sin)
        k_att = jnp.concatenate([apply_rope(ak, cos, sin), ctx['gqa_k'].astype(ak.dtype)], axis=1)
        v_att = jnp.concatenate([av, ctx['gqa_v'].astype(av.dtype)], axis=1)
        q_dif = rope_two_maps(dq, cos2, sin2)
        k_dif = jnp.concatenate([rope_two_maps(dk, cos2, sin2), ctx['diff_k'].astype(dk.dtype)], axis=1)
        v_dif = jnp.concatenate([dv, ctx['diff_v'].astype(dv.dtype)], axis=1)
        st_f = (ctx['C'][:, 0].astype(f32), ctx['n'][:, 0].astype(f32), ctx['m'][:, 0].astype(f32))
        st_b = (ctx['C'][:, 1].astype(f32), ctx['n'][:, 1].astype(f32), ctx['m'][:, 1].astype(f32))

    att_out = gqa_attention(q_att, k_att, v_att)

    lam_init = 0.8 - 0.6 * math.exp(-0.3 * layer)
    lv = lp['diff_lambda'].astype(f32)
    lam = jnp.exp(jnp.sum(lv[0] * lv[1])) - jnp.exp(jnp.sum(lv[2] * lv[3])) + lam_init
    dif_out = diff_attention(q_dif[..., :DIFF_QK_DIM], q_dif[..., DIFF_QK_DIM:],
                             k_dif[..., :DIFF_QK_DIM], k_dif[..., DIFF_QK_DIM:], v_dif, lam)
    dif_out = rms_norm(dif_out, lp['diff_gain']) * (1.0 - lam_init)

    h_ml, st_f, st_b = mlstm_bidir(mq, mk, mv, gates, st_f, st_b)
    o_gate = jax.nn.sigmoid(mo.astype(f32)).reshape(b, n, MLSTM_HEADS, HEAD_DIM)
    h_ml = rms_norm(h_ml, lp['ml_gain']) * o_gate

    cat = jnp.concatenate([att_out.reshape(b, n, ATT_W),
                           h_ml.reshape(b, n, ML_W).astype(hn.dtype),
                           dif_out.reshape(b, n, DF_W)], axis=-1)
    out = cat @ lp['w_out']
    if ctx is None:
        dt = hn.dtype
        new_ctx = (ak, av, dk, dv,
                   jnp.stack([st_f[0], st_b[0]], axis=1).astype(dt),
                   jnp.stack([st_f[1], st_b[1]], axis=1).astype(dt),
                   jnp.stack([st_f[2], st_b[2]], axis=1).astype(dt))
    else:
        new_ctx = None
    return out, new_ctx


def trunk_layer(x, cond, lp, layer, ctx):
    mod = jax.nn.silu(cond) @ lp['w_ada'] + lp['b_ada']
    sh1, sc1, g1, sh2, sc2, g2 = (m[:, None, :] for m in jnp.split(mod, 6, axis=-1))
    g = lp['norm_gain']
    hn = rms_norm(x, g[0]) * (1 + sc1) + sh1
    mix, new_ctx = mixing(hn, lp, layer, ctx)
    x = x + g1 * rms_norm(mix, g[1])
    hn = rms_norm(x, g[2]) * (1 + sc2) + sh2
    ff = jnp.square(jax.nn.relu(hn @ lp['w_ff1'])) @ lp['w_ff2']
    x = x + g2 * rms_norm(ff, g[3])
    return x, new_ctx


def setup_inputs(seed: int = 0) -> dict:
    key = jax.random.key(seed)
    ks = jax.random.split(key, 24)
    f32 = jnp.float32
    nrm = lambda k, shape, s=1.0: jax.random.normal(k, shape, f32) * s
    gate_base = jnp.array([0.0, 3.0], f32)[None, None, :, None]
    gate_scale = jnp.array([0.1, 0.5], f32)[None, None, :, None]
    return {
        'x_prompt': nrm(ks[0], (BATCH, SEQ, D_MODEL)),
        'x_sample': nrm(ks[1], (DEC_BATCH, DEC_SEQ, D_MODEL)),
        'c': nrm(ks[2], (DEC_BATCH, D_MODEL)),
        'cache_gqa_k': nrm(ks[3], (DEC_BATCH, DEPTH, PAST_LEN, ATT_KV_HEADS, HEAD_DIM)),
        'cache_gqa_v': nrm(ks[4], (DEC_BATCH, DEPTH, PAST_LEN, ATT_KV_HEADS, HEAD_DIM)),
        'cache_diff_k': nrm(ks[5], (DEC_BATCH, DEPTH, PAST_LEN, DIFF_HEADS, HEAD_DIM)),
        'cache_diff_v': nrm(ks[6], (DEC_BATCH, DEPTH, PAST_LEN, DIFF_HEADS, HEAD_DIM)),
        'state_mlstm_C': nrm(ks[7], (DEC_BATCH, DEPTH, N_DIR, MLSTM_HEADS, HEAD_DIM, HEAD_DIM), 0.1),
        'state_mlstm_n': nrm(ks[8], (DEC_BATCH, DEPTH, N_DIR, MLSTM_HEADS, HEAD_DIM), 0.1),
        'state_mlstm_m': nrm(ks[9], (DEC_BATCH, DEPTH, N_DIR, MLSTM_HEADS), 0.5),
        'c_ctx': nrm(ks[10], (D_MODEL,)),
        'w_ada': nrm(ks[11], (DEPTH, D_MODEL, 6 * D_MODEL), 0.5 * D_MODEL ** -0.5),
        'b_ada': nrm(ks[12], (DEPTH, 6 * D_MODEL), 0.02),
        'norm_gain': 1.0 + nrm(ks[13], (DEPTH, 4, D_MODEL), 0.02),
        'w_in': nrm(ks[14], (DEPTH, D_MODEL, IN_W), D_MODEL ** -0.5),
        'w_out': nrm(ks[15], (DEPTH, MIX_W, D_MODEL), MIX_W ** -0.5),
        'qk_gain': 1.0 + nrm(ks[16], (DEPTH, 2, HEAD_DIM), 0.02),
        'mlstm_gate_bias': gate_base + gate_scale * nrm(ks[17], (DEPTH, N_DIR, 2, MLSTM_HEADS)),
        'mlstm_head_gain': 1.0 + nrm(ks[18], (DEPTH, MLSTM_HEADS, HEAD_DIM), 0.02),
        'diff_lambda': nrm(ks[19], (DEPTH, 4, DIFF_QK_DIM), 0.1),
        'diff_head_gain': 1.0 + nrm(ks[20], (DEPTH, HEAD_DIM), 0.02),
        'w_ff1': nrm(ks[21], (DEPTH, D_MODEL, D_FF), D_MODEL ** -0.5),
        'w_ff2': nrm(ks[22], (DEPTH, D_FF, D_MODEL), D_FF ** -0.5),
    }


def reference(x_prompt, x_sample, c, cache_gqa_k, cache_gqa_v, cache_diff_k, cache_diff_v,
              state_mlstm_C, state_mlstm_n, state_mlstm_m, c_ctx, w_ada, b_ada, norm_gain,
              w_in, w_out, qk_gain, mlstm_gate_bias, mlstm_head_gain, diff_lambda,
              diff_head_gain, w_ff1, w_ff2):
    def layer_params(l):
        return {'w_ada': w_ada[l], 'b_ada': b_ada[l], 'norm_gain': norm_gain[l],
                'w_in': w_in[l], 'w_out': w_out[l], 'qk_gain': qk_gain[l],
                'gate_bias': mlstm_gate_bias[l], 'ml_gain': mlstm_head_gain[l],
                'diff_lambda': diff_lambda[l], 'diff_gain': diff_head_gain[l],
                'w_ff1': w_ff1[l], 'w_ff2': w_ff2[l]}

    xp = x_prompt
    collected = [[] for _ in range(7)]
    for l in range(DEPTH):
        xp, ctx_t = trunk_layer(xp, c_ctx[None, :], layer_params(l), l, None)
        for lst, t in zip(collected, ctx_t):
            lst.append(t)
    y_prompt = xp
    new_gqa_k, new_gqa_v, new_diff_k, new_diff_v, new_mlstm_C, new_mlstm_n, new_mlstm_m = (
        jnp.stack(lst, axis=1) for lst in collected)

    xs = x_sample
    for l in range(DEPTH):
        ctx = {'gqa_k': cache_gqa_k[:, l], 'gqa_v': cache_gqa_v[:, l],
               'diff_k': cache_diff_k[:, l], 'diff_v': cache_diff_v[:, l],
               'C': state_mlstm_C[:, l], 'n': state_mlstm_n[:, l], 'm': state_mlstm_m[:, l]}
        xs, _ = trunk_layer(xs, c, layer_params(l), l, ctx)
    y_sample = xs

    return (y_prompt, y_sample, new_gqa_k, new_gqa_v, new_diff_k, new_diff_v,
            new_mlstm_C, new_mlstm_n, new_mlstm_m)
```

```python
import functools
import math

import jax
import jax.numpy as jnp
from jax import lax
from jax.experimental import pallas as pl
from jax.experimental.pallas import tpu as pltpu

F32 = jnp.float32
BF16 = jnp.bfloat16

D_MODEL = 2048
BATCH = 32
SEQ = 256
DEPTH = 4
DEC_BATCH = 2
DEC_SEQ = 4096
PAST_LEN = 256
GRID_W = 64
HEAD_DIM = 128
ATT_W = 1024
ML_W = 512
DF_W = 512
ATT_HEADS = 8
ATT_KV_HEADS = 2
KV_W = 256
MLSTM_HEADS = 4
DIFF_HEADS = 4
DIFF_QK_DIM = 64
N_GATES = 16
D_FF = 8192
CHUNK = 128
ROPE_THETA = 10000.0
EPS = 1e-6
LANES = 128

T_CTX = BATCH * SEQ
T_LAT = DEC_BATCH * DEC_SEQ
KEYS_LAT = DEC_SEQ + PAST_LEN

OFF_AQ, OFF_AK, OFF_AV = 0, 1024, 1280
OFF_MQ, OFF_MK, OFF_MV, OFF_MO = 1536, 2048, 2560, 3072
OFF_DQ, OFF_DK, OFF_DV = 3584, 4096, 4608
OFF_MG = 5120
IN_W_PACKED = OFF_MG + LANES

VMEM_LIMIT = 52 * 1024 * 1024


def _dot(a, b):
    return jnp.dot(a, b, preferred_element_type=F32)


def _dot_nt(a, b):
    return lax.dot_general(a, b, (((1,), (1,)), ((), ())), preferred_element_type=F32)


def _dot_tn(a, b):
    return lax.dot_general(a, b, (((0,), (0,)), ((), ())), preferred_element_type=F32)


def _rms(x, gain):
    ms = jnp.mean(x * x, axis=-1, keepdims=True)
    return x * lax.rsqrt(ms + EPS) * gain


def _sigmoid(x):
    return 1.0 / (1.0 + jnp.exp(-x))


def _log_sigmoid(x):
    return jnp.minimum(x, 0.0) - jnp.log(1.0 + jnp.exp(-jnp.abs(x)))


def _params(n_axes):
    return pltpu.CompilerParams(dimension_semantics=("arbitrary",) * n_axes,
                                vmem_limit_bytes=VMEM_LIMIT)


MOD_TN = 512


def _mod_kernel(c_ref, w_ref, b_ref, o_ref):
    w = w_ref[...]
    reps = w.shape[1] // LANES
    rows = []
    for r in range(3):
        cb = c_ref[r]
        s = cb * _sigmoid(cb)
        sb = jnp.concatenate([s] * reps, axis=1)
        rows.append(jnp.sum(w * sb, axis=0, keepdims=True) + b_ref[...])
    rows.append(jnp.zeros((5, w.shape[1]), F32))
    o_ref[...] = jnp.concatenate(rows, axis=0)


def _modulation(cond3, w_ada, b_ada):
    cb = jnp.broadcast_to(cond3[:, :, None], (3, D_MODEL, LANES))
    n = 6 * D_MODEL
    out = pl.pallas_call(
        _mod_kernel,
        grid=(DEPTH, n // MOD_TN),
        in_specs=[
            pl.BlockSpec((3, D_MODEL, LANES), lambda l, j: (0, 0, 0)),
            pl.BlockSpec((None, D_MODEL, MOD_TN), lambda l, j: (l, 0, j)),
            pl.BlockSpec((None, 1, MOD_TN), lambda l, j: (l, 0, j)),
        ],
        out_specs=pl.BlockSpec((None, 8, MOD_TN), lambda l, j: (l, 0, j)),
        out_shape=jax.ShapeDtypeStruct((DEPTH, 8, n), F32),
        compiler_params=_params(2),
        name="adaln_mod",
    )(cb, w_ada, b_ada.reshape(DEPTH, 1, n))
    return out[:, :3].reshape(DEPTH, 3, 6, D_MODEL)


IN_TM = 256


def _rope_full(a, c, s):
    return a * c + pltpu.roll(a, 64, 1) * s


def _rope_two_maps(a, c, s):
    lane = lax.broadcasted_iota(jnp.int32, a.shape, 1)
    partner = jnp.where((lane % 64) < 32, pltpu.roll(a, 96, 1), pltpu.roll(a, 32, 1))
    return a * c + partner * s


def _in_proj_kernel(latent, *refs):
    if latent:
        (x_ref, mod_ref, g_ref, w_ref, qkg_ref, gb_ref, c128_ref, s128_ref, c64_ref, s64_ref,
         q_ref, k_ref, v_ref, mq_ref, mk_ref, mv_ref, og_ref, dq_ref, dk_ref, dv_ref, gt_ref) = refs
    else:
        (x_ref, mod_ref, g_ref, w_ref, qkg_ref, gb_ref,
         q_ref, k_ref, v_ref, mq_ref, mk_ref, mv_ref, og_ref, dq_ref, dk_ref, dv_ref, gt_ref) = refs

    x = x_ref[...]
    hn = _rms(x, g_ref[0:1, :]) * (1.0 + mod_ref[1:2, :]) + mod_ref[0:1, :]
    hnb = hn.astype(BF16)

    def proj(off, width):
        return _dot(hnb, w_ref[:, off:off + width])

    q_scale = HEAD_DIM ** -0.5
    aq = proj(OFF_AQ, ATT_W)
    for h in range(ATT_HEADS):
        a = _rms(aq[:, h * HEAD_DIM:(h + 1) * HEAD_DIM], qkg_ref[0:1, :])
        if latent:
            a = _rope_full(a, c128_ref[...], s128_ref[...])
        q_ref[:, h * HEAD_DIM:(h + 1) * HEAD_DIM] = (a * q_scale).astype(q_ref.dtype)

    ak = proj(OFF_AK, KV_W)
    for h in range(ATT_KV_HEADS):
        a = _rms(ak[:, h * HEAD_DIM:(h + 1) * HEAD_DIM], qkg_ref[1:2, :])
        if latent:
            a = _rope_full(a, c128_ref[...], s128_ref[...])
        k_ref[:, h * HEAD_DIM:(h + 1) * HEAD_DIM] = a.astype(k_ref.dtype)

    v_ref[...] = proj(OFF_AV, KV_W).astype(v_ref.dtype)
    mq_ref[...] = proj(OFF_MQ, ML_W).astype(mq_ref.dtype)
    mk_ref[...] = (proj(OFF_MK, ML_W) * (HEAD_DIM ** -0.5)).astype(mk_ref.dtype)
    mv_ref[...] = proj(OFF_MV, ML_W).astype(mv_ref.dtype)
    og_ref[...] = _sigmoid(proj(OFF_MO, ML_W)).astype(og_ref.dtype)

    dq = proj(OFF_DQ, DF_W)
    dk = proj(OFF_DK, DF_W)
    dq_scale = DIFF_QK_DIM ** -0.5
    for h in range(DIFF_HEADS):
        sl = slice(h * HEAD_DIM, (h + 1) * HEAD_DIM)
        a, b = dq[:, sl], dk[:, sl]
        if latent:
            a = _rope_two_maps(a, c64_ref[...], s64_ref[...])
            b = _rope_two_maps(b, c64_ref[...], s64_ref[...])
        dq_ref[:, sl] = (a * dq_scale).astype(dq_ref.dtype)
        dk_ref[:, sl] = b.astype(dk_ref.dtype)
    dv_ref[...] = proj(OFF_DV, DF_W).astype(dv_ref.dtype)

    g = proj(OFF_MG, LANES) + gb_ref[...]
    lane = lax.broadcasted_iota(jnp.int32, g.shape, 1)
    gt_ref[...] = jnp.where((lane & 4) != 0, _log_sigmoid(g), g)


def _in_proj(x, mod_l, gains, w_in_l, qk_gain_l, gate_bias_row, rope, latent):
    t = x.shape[0]
    tm = IN_TM
    n_tiles = t // tm
    per_seq = DEC_SEQ // tm
    if latent:
        cond_of = lambda i: 1 + i // per_seq
    else:
        cond_of = lambda i: 0
    row = lambda width: pl.BlockSpec((tm, width), lambda i: (i, 0))
    in_specs = [
        row(D_MODEL),
        pl.BlockSpec((None, 6, D_MODEL), lambda i: (cond_of(i), 0, 0)),
        pl.BlockSpec((4, D_MODEL), lambda i: (0, 0)),
        pl.BlockSpec((D_MODEL, IN_W_PACKED), lambda i: (0, 0), pipeline_mode=pl.Buffered(1)),
        pl.BlockSpec((2, HEAD_DIM), lambda i: (0, 0)),
        pl.BlockSpec((1, LANES), lambda i: (0, 0)),
    ]
    args = [x, mod_l, gains, w_in_l, qk_gain_l, gate_bias_row]
    if latent:
        in_specs += [pl.BlockSpec((tm, LANES), lambda i: (i % per_seq, 0))] * 4
        args += list(rope)
    kv_dt = BF16 if latent else F32
    widths_dtypes = [(ATT_W, BF16), (KV_W, kv_dt), (KV_W, kv_dt),
                     (ML_W, BF16), (ML_W, BF16), (ML_W, BF16), (ML_W, BF16),
                     (DF_W, BF16), (DF_W, kv_dt), (DF_W, kv_dt), (LANES, F32)]
    return pl.pallas_call(
        functools.partial(_in_proj_kernel, latent),
        grid=(n_tiles,),
        in_specs=in_specs,
        out_specs=[row(w) for w, _ in widths_dtypes],
        out_shape=[jax.ShapeDtypeStruct((t, w), dt) for w, dt in widths_dtypes],
        compiler_params=_params(1),
        name="in_proj_lat" if latent else "in_proj_ctx",
    )(*args)


def _softmax_pv(q, k, v):
    s = _dot_nt(q, k)
    e = jnp.exp(s - jnp.max(s, axis=-1, keepdims=True))
    return _dot(e.astype(BF16), v), jnp.sum(e, axis=-1, keepdims=True)


def _diff_lambda(lam_ref, lam_init):
    lv = lam_ref[...]
    a = jnp.sum(lv[0:1, :] * lv[1:2, :], axis=-1, keepdims=True)
    b = jnp.sum(lv[2:3, :] * lv[3:4, :], axis=-1, keepdims=True)
    return jnp.exp(a) - jnp.exp(b) + lam_init


def _diff_head(q, k, v, lam, gain, lam_init):
    lane = lax.broadcasted_iota(jnp.int32, q.shape, 1)
    zero = jnp.zeros_like(q)
    qs = jnp.concatenate([jnp.where(lane < DIFF_QK_DIM, q, zero),
                          jnp.where(lane >= DIFF_QK_DIM, q, zero)], axis=0)
    o, l = _softmax_pv(qs, k, v)
    o = o / l
    m = q.shape[0]
    d = o[:m] - lam * o[m:]
    return _rms(d, gain) * (1.0 - lam_init)


def _tri(lower):
    r = lax.broadcasted_iota(jnp.int32, (CHUNK, CHUNK), 0)
    c = lax.broadcasted_iota(jnp.int32, (CHUNK, CHUNK), 1)
    return (c <= r) if lower else (c >= r)


def _cumsum_rows(tri_b, g):
    g1 = g.astype(BF16)
    r1 = g - g1.astype(F32)
    g2 = r1.astype(BF16)
    g3 = (r1 - g2.astype(F32)).astype(BF16)
    return _dot(tri_b, g1) + _dot(tri_b, g2) + _dot(tri_b, g3)


def _gate_forms(g, tri_b):
    b = _cumsum_rows(tri_b, g)
    return b, b.T


def _mlstm_chunk(q, k, v, bcol, brow, icol, irow, mask, last, C, n, m):
    logw = jnp.where(mask, bcol - brow + irow, -jnp.inf)
    inter = bcol + m
    m_row = jnp.maximum(inter, jnp.max(logw, axis=1, keepdims=True))
    sc = _dot_nt(q, k) * jnp.exp(logw - m_row)
    a_int = jnp.exp(inter - m_row)
    num = _dot(sc.astype(BF16), v) + a_int * _dot(q, C.astype(BF16))
    den = (jnp.sum(sc, axis=1, keepdims=True)
           + a_int * jnp.sum(q.astype(F32) * n, axis=1, keepdims=True))
    h = num / jnp.maximum(jnp.abs(den), jnp.exp(-m_row))
    m_new = m_row[last:last + 1, :]
    btot = bcol[last:last + 1, :]
    w_end = jnp.exp(btot - bcol + icol - m_new)
    decay = jnp.exp(btot + m - m_new)
    kw = k.astype(F32) * w_end
    C_new = decay * C + _dot_tn(kw.astype(BF16), v)
    n_new = decay * n + jnp.sum(kw, axis=0, keepdims=True)
    return h, C_new, n_new, m_new


def _gate_cols(d, h):
    return d * 8 + h, d * 8 + 4 + h


def _ctx_mix_kernel(lam_init, q_ref, k_ref, v_ref, mq_ref, mk_ref, mv_ref, og_ref, dq_ref, dk_ref,
                    dv_ref, gt_ref, mlg_ref, dfg_ref, lam_ref,
                    att_ref, ml_ref, dif_ref, C_ref, n_ref, m_ref):
    n_tok = SEQ
    rep = ATT_HEADS // ATT_KV_HEADS
    for g in range(ATT_KV_HEADS):
        sl = slice(g * HEAD_DIM, (g + 1) * HEAD_DIM)
        kb = k_ref[:, sl].astype(BF16)
        vb = v_ref[:, sl].astype(BF16)
        qs = jnp.concatenate(
            [q_ref[:, (g * rep + r) * HEAD_DIM:(g * rep + r + 1) * HEAD_DIM] for r in range(rep)], axis=0)
        o, l = _softmax_pv(qs, kb, vb)
        o = o / l
        for r in range(rep):
            hh = g * rep + r
            att_ref[:, hh * HEAD_DIM:(hh + 1) * HEAD_DIM] = o[r * n_tok:(r + 1) * n_tok].astype(att_ref.dtype)

    lam = _diff_lambda(lam_ref, lam_init)
    for h in range(DIFF_HEADS):
        sl = slice(h * HEAD_DIM, (h + 1) * HEAD_DIM)
        d = _diff_head(dq_ref[:, sl], dk_ref[:, sl].astype(BF16), dv_ref[:, sl].astype(BF16),
                       lam, dfg_ref[...], lam_init)
        dif_ref[:, sl] = d.astype(dif_ref.dtype)

    n_chunks = n_tok // CHUNK
    masks = (_tri(True), _tri(False))
    tris = tuple(jnp.where(mk, 1.0, 0.0).astype(BF16) for mk in masks)
    forms = []
    for c in range(n_chunks):
        g = gt_ref[c * CHUNK:(c + 1) * CHUNK, :]
        forms.append((g, g.T) + tuple(_gate_forms(g, tris[d]) for d in range(2)))
    for h in range(MLSTM_HEADS):
        sl = slice(h * HEAD_DIM, (h + 1) * HEAD_DIM)
        hsum = [None] * n_chunks
        for d in range(2):
            ci, cf = _gate_cols(d, h)
            C = jnp.zeros((HEAD_DIM, HEAD_DIM), F32)
            n = jnp.zeros((1, HEAD_DIM), F32)
            m = jnp.full((1, 1), -jnp.inf, F32)
            order = range(n_chunks) if d == 0 else range(n_chunks - 1, -1, -1)
            for c in order:
                g, gT, f0, f1 = forms[c]
                b, bT = f0 if d == 0 else f1
                rows = slice(c * CHUNK, (c + 1) * CHUNK)
                hc, C, n, m = _mlstm_chunk(
                    mq_ref[rows, sl], mk_ref[rows, sl], mv_ref[rows, sl],
                    b[:, cf:cf + 1], bT[cf:cf + 1, :], g[:, ci:ci + 1], gT[ci:ci + 1, :],
                    masks[d], CHUNK - 1 if d == 0 else 0, C, n, m)
                hsum[c] = hc if hsum[c] is None else hsum[c] + hc
            C_ref[d, h] = C
            n_ref[d * MLSTM_HEADS + h:d * MLSTM_HEADS + h + 1, :] = n
            m_ref[d * MLSTM_HEADS + h:d * MLSTM_HEADS + h + 1, :] = jnp.broadcast_to(m, (1, LANES))
        for c in range(n_chunks):
            rows = slice(c * CHUNK, (c + 1) * CHUNK)
            y = _rms(hsum[c], mlg_ref[h:h + 1, :]) * og_ref[rows, sl].astype(F32)
            ml_ref[rows, sl] = y.astype(ml_ref.dtype)


def _ctx_mix(layer, q, k, v, mq, mk, mv, og, dq, dk, dv, gt, ml_gain, diff_gain, diff_lambda):
    lam_init = 0.8 - 0.6 * math.exp(-0.3 * layer)
    row = lambda width: pl.BlockSpec((SEQ, width), lambda b: (b, 0))
    full = lambda a: pl.BlockSpec(a.shape, lambda b: (0,) * a.ndim)
    nh = 2 * MLSTM_HEADS
    return pl.pallas_call(
        functools.partial(_ctx_mix_kernel, lam_init),
        grid=(BATCH,),
        in_specs=[row(ATT_W), row(KV_W), row(KV_W), row(ML_W), row(ML_W), row(ML_W), row(ML_W),
                  row(DF_W), row(DF_W), row(DF_W), row(LANES),
                  full(ml_gain), full(diff_gain), full(diff_lambda)],
        out_specs=[row(ATT_W), row(ML_W), row(DF_W),
                   pl.BlockSpec((None, 2, MLSTM_HEADS, HEAD_DIM, HEAD_DIM), lambda b: (b, 0, 0, 0, 0)),
                   pl.BlockSpec((None, nh, HEAD_DIM), lambda b: (b, 0, 0)),
                   pl.BlockSpec((None, nh, LANES), lambda b: (b, 0, 0))],
        out_shape=[jax.ShapeDtypeStruct((T_CTX, ATT_W), BF16),
                   jax.ShapeDtypeStruct((T_CTX, ML_W), BF16),
                   jax.ShapeDtypeStruct((T_CTX, DF_W), BF16),
                   jax.ShapeDtypeStruct((BATCH, 2, MLSTM_HEADS, HEAD_DIM, HEAD_DIM), F32),
                   jax.ShapeDtypeStruct((BATCH, nh, HEAD_DIM), F32),
                   jax.ShapeDtypeStruct((BATCH, nh, LANES), F32)],
        compiler_params=_params(1),
        name="ctx_mix",
    )(q, k, v, mq, mk, mv, og, dq, dk, dv, gt, ml_gain, diff_gain, diff_lambda)


GQA_TQ = 128
DIFF_TQ = 256


def _lat_gqa_kernel(q_ref, k_ref, v_ref, o_ref):
    rep = ATT_HEADS // ATT_KV_HEADS
    tq = q_ref.shape[0]
    qs = jnp.concatenate([q_ref[:, r * HEAD_DIM:(r + 1) * HEAD_DIM] for r in range(rep)], axis=0)
    o, l = _softmax_pv(qs, k_ref[...], v_ref[...])
    o = o / l
    for r in range(rep):
        o_ref[:, r * HEAD_DIM:(r + 1) * HEAD_DIM] = o[r * tq:(r + 1) * tq].astype(o_ref.dtype)


def _lat_gqa(q, k_full, v_full):
    rep = ATT_HEADS // ATT_KV_HEADS
    nq = DEC_SEQ // GQA_TQ
    qspec = pl.BlockSpec((GQA_TQ, rep * HEAD_DIM), lambda b, g, i: (b * nq + i, g))
    kvspec = pl.BlockSpec((None, KEYS_LAT, HEAD_DIM), lambda b, g, i: (b, 0, g))
    return pl.pallas_call(
        _lat_gqa_kernel,
        grid=(DEC_BATCH, ATT_KV_HEADS, nq),
        in_specs=[qspec, kvspec, kvspec],
        out_specs=qspec,
        out_shape=jax.ShapeDtypeStruct((T_LAT, ATT_W), BF16),
        compiler_params=_params(3),
        name="lat_gqa",
    )(q, k_full, v_full)


def _lat_diff_kernel(lam_init, q_ref, k_ref, v_ref, dfg_ref, lam_ref, o_ref):
    lam = _diff_lambda(lam_ref, lam_init)
    d = _diff_head(q_ref[...], k_ref[...], v_ref[...], lam, dfg_ref[...], lam_init)
    o_ref[...] = d.astype(o_ref.dtype)


def _lat_diff(layer, dq, dk_full, dv_full, diff_gain, diff_lambda):
    lam_init = 0.8 - 0.6 * math.exp(-0.3 * layer)
    nq = DEC_SEQ // DIFF_TQ
    qspec = pl.BlockSpec((DIFF_TQ, HEAD_DIM), lambda b, h, i: (b * nq + i, h))
    kvspec = pl.BlockSpec((None, KEYS_LAT, HEAD_DIM), lambda b, h, i: (b, 0, h))
    full = lambda a: pl.BlockSpec(a.shape, lambda b, h, i: (0,) * a.ndim)
    return pl.pallas_call(
        functools.partial(_lat_diff_kernel, lam_init),
        grid=(DEC_BATCH, DIFF_HEADS, nq),
        in_specs=[qspec, kvspec, kvspec, full(diff_gain), full(diff_lambda)],
        out_specs=qspec,
        out_shape=jax.ShapeDtypeStruct((T_LAT, DF_W), BF16),
        compiler_params=_params(3),
        name="lat_diff",
    )(dq, dk_full, dv_full, diff_gain, diff_lambda)


def _lat_mlstm_kernel(q_ref, k_ref, v_ref, og_ref, gt_ref, C0_ref, n0_ref, m0_ref, mlg_ref,
                      o_ref, hf_ref, hb_ref):
    h = pl.program_id(1)
    n_chunks = DEC_SEQ // CHUNK
    masks = (_tri(True), _tri(False))
    tris = tuple(jnp.where(mk, 1.0, 0.0).astype(BF16) for mk in masks)
    lane_col = lax.broadcasted_iota(jnp.int32, (CHUNK, LANES), 1)
    lane_row = lax.broadcasted_iota(jnp.int32, (LANES, CHUNK), 0)

    def pick(g, gT, b, bT, d):
        ci = d * 8 + h
        cf = d * 8 + 4 + h
        col = lambda a, c: jnp.sum(jnp.where(lane_col == c, a, 0.0), axis=1, keepdims=True)
        rowf = lambda a, c: jnp.sum(jnp.where(lane_row == c, a, 0.0), axis=0, keepdims=True)
        return col(b, cf), rowf(bT, cf), col(g, ci), rowf(gT, ci)

    def step(t, carry):
        Cf, nf, mf, Cb, nb, mb = carry
        outs = []
        for d, (C, n, m) in enumerate(((Cf, nf, mf), (Cb, nb, mb))):
            c = t if d == 0 else n_chunks - 1 - t
            rows = pl.ds(pl.multiple_of(c * CHUNK, CHUNK), CHUNK)
            g = gt_ref[rows, :]
            b, bT = _gate_forms(g, tris[d])
            bcol, brow, icol, irow = pick(g, g.T, b, bT, d)
            hc, C, n, m = _mlstm_chunk(q_ref[rows, :], k_ref[rows, :], v_ref[rows, :],
                                       bcol, brow, icol, irow, masks[d],
                                       CHUNK - 1 if d == 0 else 0, C, n, m)
            (hf_ref if d == 0 else hb_ref)[rows, :] = hc
            outs += [C, n, m]
        return tuple(outs)

    init = (C0_ref[0], n0_ref[0], m0_ref[0][:, 0:1], C0_ref[1], n0_ref[1], m0_ref[1][:, 0:1])
    lax.fori_loop(0, n_chunks, step, init)

    gain_rows = mlg_ref[...]
    sub = lax.broadcasted_iota(jnp.int32, gain_rows.shape, 0)
    gain = jnp.sum(jnp.where(sub == h, gain_rows, 0.0), axis=0, keepdims=True)
    y = _rms(hf_ref[...] + hb_ref[...], gain) * og_ref[...].astype(F32)
    o_ref[...] = y.astype(o_ref.dtype)


def _lat_mlstm(layer, mq, mk, mv, og, gt, state_C, state_n6, state_m6, ml_gain):
    hspec = pl.BlockSpec((DEC_SEQ, HEAD_DIM), lambda b, h: (b, h))
    st = lambda last2: pl.BlockSpec((None, None, 2, None) + last2, lambda b, h: (b, layer, 0, h, 0, 0))
    return pl.pallas_call(
        _lat_mlstm_kernel,
        grid=(DEC_BATCH, MLSTM_HEADS),
        in_specs=[hspec, hspec, hspec, hspec,
                  pl.BlockSpec((DEC_SEQ, LANES), lambda b, h: (b, 0)),
                  st((HEAD_DIM, HEAD_DIM)), st((1, HEAD_DIM)), st((1, LANES)),
                  pl.BlockSpec(ml_gain.shape, lambda b, h: (0, 0))],
        out_specs=hspec,
        out_shape=jax.ShapeDtypeStruct((T_LAT, ML_W), BF16),
        scratch_shapes=[pltpu.VMEM((DEC_SEQ, HEAD_DIM), F32), pltpu.VMEM((DEC_SEQ, HEAD_DIM), F32)],
        compiler_params=_params(2),
        name="lat_mlstm",
    )(mq, mk, mv, og, gt, state_C, state_n6, state_m6, ml_gain)


OUT_TM = 512
FFN_TM = 512
FFN_TF = 1024


def _cond_index(latent, tm):
    per_seq = DEC_SEQ // tm
    if latent:
        return lambda i: 1 + i // per_seq
    return lambda i: 0


def _out_proj_kernel(att_ref, ml_ref, dif_ref, w_ref, x_ref, mod_ref, g_ref, o_ref):
    mix = (_dot(att_ref[...], w_ref[0:ATT_W, :])
           + _dot(ml_ref[...], w_ref[ATT_W:ATT_W + ML_W, :])
           + _dot(dif_ref[...], w_ref[ATT_W + ML_W:, :]))
    o_ref[...] = x_ref[...] + mod_ref[2:3, :] * _rms(mix, g_ref[1:2, :])


def _out_proj(att, ml, dif, w_out_l, x, mod_l, gains, latent):
    t = x.shape[0]
    tm = OUT_TM
    cond_of = _cond_index(latent, tm)
    row = lambda width: pl.BlockSpec((tm, width), lambda i: (i, 0))
    return pl.pallas_call(
        _out_proj_kernel,
        grid=(t // tm,),
        in_specs=[row(ATT_W), row(ML_W), row(DF_W),
                  pl.BlockSpec((D_MODEL, D_MODEL), lambda i: (0, 0), pipeline_mode=pl.Buffered(1)),
                  row(D_MODEL),
                  pl.BlockSpec((None, 6, D_MODEL), lambda i: (cond_of(i), 0, 0)),
                  pl.BlockSpec((4, D_MODEL), lambda i: (0, 0))],
        out_specs=row(D_MODEL),
        out_shape=jax.ShapeDtypeStruct((t, D_MODEL), F32),
        compiler_params=_params(1),
        name="out_proj",
    )(att, ml, dif, w_out_l, x, mod_l, gains)


def _ffn_kernel(x_ref, mod_ref, g_ref, w1_ref, w2_ref, o_ref, hn_ref):
    f = pl.program_id(1)

    @pl.when(f == 0)
    def _():
        hn = _rms(x_ref[...], g_ref[2:3, :]) * (1.0 + mod_ref[4:5, :]) + mod_ref[3:4, :]
        hn_ref[...] = hn.astype(hn_ref.dtype)

    h = jnp.maximum(_dot(hn_ref[...], w1_ref[...]), 0.0)
    part = _dot((h * h).astype(BF16), w2_ref[...])

    @pl.when(f == 0)
    def _():
        o_ref[...] = part

    @pl.when(f > 0)
    def _():
        o_ref[...] += part

    @pl.when(f == pl.num_programs(1) - 1)
    def _():
        o_ref[...] = x_ref[...] + mod_ref[5:6, :] * _rms(o_ref[...], g_ref[3:4, :])


def _ffn(x, mod_l, gains, w1_l, w2_l, latent):
    t = x.shape[0]
    tm, tf = FFN_TM, FFN_TF
    cond_of = _cond_index(latent, tm)
    return pl.pallas_call(
        _ffn_kernel,
        grid=(t // tm, D_FF // tf),
        in_specs=[pl.BlockSpec((tm, D_MODEL), lambda i, f: (i, 0)),
                  pl.BlockSpec((None, 6, D_MODEL), lambda i, f: (cond_of(i), 0, 0)),
                  pl.BlockSpec((4, D_MODEL), lambda i, f: (0, 0)),
                  pl.BlockSpec((D_MODEL, tf), lambda i, f: (0, f)),
                  pl.BlockSpec((tf, D_MODEL), lambda i, f: (f, 0))],
        out_specs=pl.BlockSpec((tm, D_MODEL), lambda i, f: (i, 0)),
        out_shape=jax.ShapeDtypeStruct((t, D_MODEL), F32),
        scratch_shapes=[pltpu.VMEM((tm, D_MODEL), BF16)],
        compiler_params=_params(2),
        name="ffn",
    )(x, mod_l, gains, w1_l, w2_l)


def _rope_tables():
    def angles(dim):
        rows = DEC_SEQ // GRID_W
        row_idx = jnp.repeat(jnp.arange(rows), GRID_W).astype(F32)
        col_idx = jnp.tile(jnp.arange(GRID_W), rows).astype(F32)
        n_freq = dim // 4
        inv = ROPE_THETA ** (-jnp.arange(n_freq, dtype=F32) / n_freq)
        ang = jnp.concatenate([row_idx[:, None] * inv, col_idx[:, None] * inv], axis=-1)
        return jnp.cos(ang), jnp.sin(ang)

    c, s = angles(HEAD_DIM)
    c2, s2 = angles(DIFF_QK_DIM)
    c128 = jnp.concatenate([c, c], axis=-1)
    s128 = jnp.concatenate([-s, s], axis=-1)
    c64 = jnp.concatenate([c2, c2, c2, c2], axis=-1)
    s64 = jnp.concatenate([-s2, s2, -s2, s2], axis=-1)
    return c128, s128, c64, s64


def _pack_w_in(w_in):
    g0 = OFF_MO + ML_W
    pad = jnp.zeros(w_in.shape[:2] + (LANES - N_GATES,), w_in.dtype)
    packed = jnp.concatenate([w_in[..., :g0], w_in[..., g0 + N_GATES:], w_in[..., g0:g0 + N_GATES], pad], axis=-1)
    return packed.astype(BF16)


def kernel(x_prompt, x_sample, c, cache_gqa_k, cache_gqa_v, cache_diff_k, cache_diff_v, state_mlstm_C,
           state_mlstm_n, state_mlstm_m, c_ctx, w_ada, b_ada, norm_gain, w_in, w_out, qk_gain,
           mlstm_gate_bias, mlstm_head_gain, diff_lambda, diff_head_gain, w_ff1, w_ff2):
    cond3 = jnp.concatenate([c_ctx[None, :], c], axis=0)
    mod = _modulation(cond3, w_ada, b_ada)

    w_in_p = _pack_w_in(w_in)
    w_out_b = w_out.astype(BF16)
    w1_b = w_ff1.astype(BF16)
    w2_b = w_ff2.astype(BF16)
    gate_bias = jnp.pad(mlstm_gate_bias.reshape(DEPTH, 1, N_GATES), ((0, 0), (0, 0), (0, LANES - N_GATES)))
    diff_gain = diff_head_gain.reshape(DEPTH, 1, HEAD_DIM)
    rope = _rope_tables()

    state_n6 = state_mlstm_n.reshape(DEC_BATCH, DEPTH, 2, MLSTM_HEADS, 1, HEAD_DIM)
    state_m6 = jnp.broadcast_to(state_mlstm_m[..., None, None], (DEC_BATCH, DEPTH, 2, MLSTM_HEADS, 1, LANES))

    xp = x_prompt.reshape(T_CTX, D_MODEL)
    xs = x_sample.reshape(T_LAT, D_MODEL)
    new = [[] for _ in range(7)]

    for l in range(DEPTH):
        (q, k, v, mq, mk, mv, og, dq, dk, dv, gt) = _in_proj(
            xp, mod[l], norm_gain[l], w_in_p[l], qk_gain[l], gate_bias[l], None, False)
        att, ml, dif, C, n, m = _ctx_mix(l, q, k, v, mq, mk, mv, og, dq, dk, dv, gt,
                                         mlstm_head_gain[l], diff_gain[l], diff_lambda[l])
        xp = _out_proj(att, ml, dif, w_out_b[l], xp, mod[l], norm_gain[l], False)
        xp = _ffn(xp, mod[l], norm_gain[l], w1_b[l], w2_b[l], False)
        new[0].append(k.reshape(BATCH, SEQ, ATT_KV_HEADS, HEAD_DIM))
        new[1].append(v.reshape(BATCH, SEQ, ATT_KV_HEADS, HEAD_DIM))
        new[2].append(dk.reshape(BATCH, SEQ, DIFF_HEADS, HEAD_DIM))
        new[3].append(dv.reshape(BATCH, SEQ, DIFF_HEADS, HEAD_DIM))
        new[4].append(C)
        new[5].append(n.reshape(BATCH, 2, MLSTM_HEADS, HEAD_DIM))
        new[6].append(m[:, :, 0].reshape(BATCH, 2, MLSTM_HEADS))

        (q, k, v, mq, mk, mv, og, dq, dk, dv, gt) = _in_proj(
            xs, mod[l], norm_gain[l], w_in_p[l], qk_gain[l], gate_bias[l], rope, True)
        as_keys = lambda new_rows, cache, width: jnp.concatenate(
            [new_rows.reshape(DEC_BATCH, DEC_SEQ, width),
             cache[:, l].reshape(DEC_BATCH, PAST_LEN, width).astype(BF16)], axis=1)
        att = _lat_gqa(q, as_keys(k, cache_gqa_k, KV_W), as_keys(v, cache_gqa_v, KV_W))
        dif = _lat_diff(l, dq, as_keys(dk, cache_diff_k, DF_W), as_keys(dv, cache_diff_v, DF_W),
                        diff_gain[l], diff_lambda[l])
        ml = _lat_mlstm(l, mq, mk, mv, og, gt, state_mlstm_C, state_n6, state_m6, mlstm_head_gain[l])
        xs = _out_proj(att, ml, dif, w_out_b[l], xs, mod[l], norm_gain[l], True)
        xs = _ffn(xs, mod[l], norm_gain[l], w1_b[l], w2_b[l], True)

    outs = [jnp.stack(lst, axis=1) for lst in new]
    return (xp.reshape(BATCH, SEQ, D_MODEL), xs.reshape(DEC_BATCH, DEC_SEQ, D_MODEL), *outs)
```

```python
import functools
import math

import jax
import jax.numpy as jnp
from jax import lax
from jax.experimental import pallas as pl
from jax.experimental.pallas import tpu as pltpu

F32 = jnp.float32
BF16 = jnp.bfloat16

D_MODEL = 2048
BATCH = 32
SEQ = 256
DEPTH = 4
DEC_BATCH = 2
DEC_SEQ = 4096
PAST_LEN = 256
GRID_W = 64
HEAD_DIM = 128
ATT_W = 1024
ML_W = 512
DF_W = 512
ATT_HEADS = 8
ATT_KV_HEADS = 2
KV_W = 256
MLSTM_HEADS = 4
DIFF_HEADS = 4
DIFF_QK_DIM = 64
N_GATES = 16
D_FF = 8192
CHUNK = 128
ROPE_THETA = 10000.0
EPS = 1e-6
LOG2E = 1.4426950408889634
LANES = 128

T_CTX = BATCH * SEQ
T_LAT = DEC_BATCH * DEC_SEQ
KEYS_LAT = DEC_SEQ + PAST_LEN

OFF_AQ, OFF_AK, OFF_AV = 0, 1024, 1280
OFF_MQ, OFF_MK, OFF_MV, OFF_MO = 1536, 2048, 2560, 3072
OFF_DQ, OFF_DK, OFF_DV = 3584, 4096, 4608
OFF_MG = 5120
IN_W_PACKED = OFF_MG + LANES

VMEM_LIMIT = 52 * 1024 * 1024


def _dot(a, b):
    return jnp.dot(a, b, preferred_element_type=F32)


def _dot_nt(a, b):
    return lax.dot_general(a, b, (((1,), (1,)), ((), ())), preferred_element_type=F32)


def _dot_tn(a, b):
    return lax.dot_general(a, b, (((0,), (0,)), ((), ())), preferred_element_type=F32)


def _rms(x, gain):
    ms = jnp.mean(x * x, axis=-1, keepdims=True)
    return x * lax.rsqrt(ms + EPS) * gain


def _sigmoid(x):
    return 1.0 / (1.0 + jnp.exp(-x))


def _log_sigmoid(x):
    return jnp.minimum(x, 0.0) - jnp.log(1.0 + jnp.exp(-jnp.abs(x)))


def _params(n_axes):
    return pltpu.CompilerParams(dimension_semantics=("arbitrary",) * n_axes,
                                vmem_limit_bytes=VMEM_LIMIT)


MOD_TN = 512


def _mod_kernel(c_ref, w_ref, b_ref, o_ref, s_ref):
    @pl.when((pl.program_id(0) == 0) & (pl.program_id(1) == 0))
    def _():
        cb = c_ref[...]
        s_ref[...] = cb * _sigmoid(cb)

    w = w_ref[...]
    reps = w.shape[1] // LANES
    rows = []
    for r in range(3):
        sb = jnp.concatenate([s_ref[r]] * reps, axis=1)
        rows.append(jnp.sum(w * sb, axis=0, keepdims=True) + b_ref[...])
    rows.append(jnp.zeros((5, w.shape[1]), F32))
    o_ref[...] = jnp.concatenate(rows, axis=0)


def _modulation(cond3, w_ada, b_ada):
    cb = jnp.broadcast_to(cond3[:, :, None], (3, D_MODEL, LANES))
    n = 6 * D_MODEL
    out = pl.pallas_call(
        _mod_kernel,
        grid=(DEPTH, n // MOD_TN),
        in_specs=[
            pl.BlockSpec((3, D_MODEL, LANES), lambda l, j: (0, 0, 0)),
            pl.BlockSpec((None, D_MODEL, MOD_TN), lambda l, j: (l, 0, j)),
            pl.BlockSpec((None, 1, MOD_TN), lambda l, j: (l, 0, j)),
        ],
        out_specs=pl.BlockSpec((None, 8, MOD_TN), lambda l, j: (l, 0, j)),
        out_shape=jax.ShapeDtypeStruct((DEPTH, 8, n), F32),
        scratch_shapes=[pltpu.VMEM((3, D_MODEL, LANES), F32)],
        compiler_params=_params(2),
        name="adaln_mod",
    )(cb, w_ada, b_ada.reshape(DEPTH, 1, n))
    return out[:, :3].reshape(DEPTH, 3, 6, D_MODEL)


IN_TM = 256


def _rope_full(a, c, s):
    return a * c + pltpu.roll(a, 64, 1) * s


def _rope_two_maps(a, c, s):
    lane = lax.broadcasted_iota(jnp.int32, a.shape, 1)
    partner = jnp.where((lane % 64) < 32, pltpu.roll(a, 96, 1), pltpu.roll(a, 32, 1))
    return a * c + partner * s


def _in_proj_kernel(latent, *refs):
    if latent:
        (x_ref, mod_ref, g_ref, w_ref, qkg_ref, gb_ref, c128_ref, s128_ref, c64_ref, s64_ref,
         q_ref, k_ref, v_ref, mq_ref, mk_ref, mv_ref, og_ref, dq_ref, dk_ref, dv_ref, gt_ref, gT_ref) = refs
    else:
        (x_ref, mod_ref, g_ref, w_ref, qkg_ref, gb_ref,
         q_ref, k_ref, v_ref, mq_ref, mk_ref, mv_ref, og_ref, dq_ref, dk_ref, dv_ref, gt_ref, gT_ref) = refs

    x = x_ref[...]
    hn = _rms(x, g_ref[0:1, :]) * (1.0 + mod_ref[1:2, :]) + mod_ref[0:1, :]
    hnb = hn.astype(BF16)

    def proj(off, width):
        return _dot(hnb, w_ref[:, off:off + width])

    q_scale = HEAD_DIM ** -0.5 * LOG2E
    aq = proj(OFF_AQ, ATT_W)
    for h in range(ATT_HEADS):
        a = _rms(aq[:, h * HEAD_DIM:(h + 1) * HEAD_DIM], qkg_ref[0:1, :])
        if latent:
            a = _rope_full(a, c128_ref[...], s128_ref[...])
        q_ref[:, h * HEAD_DIM:(h + 1) * HEAD_DIM] = (a * q_scale).astype(q_ref.dtype)

    ak = proj(OFF_AK, KV_W)
    for h in range(ATT_KV_HEADS):
        a = _rms(ak[:, h * HEAD_DIM:(h + 1) * HEAD_DIM], qkg_ref[1:2, :])
        if latent:
            a = _rope_full(a, c128_ref[...], s128_ref[...])
        k_ref[:, h * HEAD_DIM:(h + 1) * HEAD_DIM] = a.astype(k_ref.dtype)

    v_ref[...] = proj(OFF_AV, KV_W).astype(v_ref.dtype)
    mq_ref[...] = proj(OFF_MQ, ML_W).astype(mq_ref.dtype)
    mk_ref[...] = (proj(OFF_MK, ML_W) * (HEAD_DIM ** -0.5)).astype(mk_ref.dtype)
    mv_ref[...] = proj(OFF_MV, ML_W).astype(mv_ref.dtype)
    og_ref[...] = _sigmoid(proj(OFF_MO, ML_W)).astype(og_ref.dtype)

    dq = proj(OFF_DQ, DF_W)
    dk = proj(OFF_DK, DF_W)
    dq_scale = DIFF_QK_DIM ** -0.5 * LOG2E
    for h in range(DIFF_HEADS):
        sl = slice(h * HEAD_DIM, (h + 1) * HEAD_DIM)
        a, b = dq[:, sl], dk[:, sl]
        if latent:
            a = _rope_two_maps(a, c64_ref[...], s64_ref[...])
            b = _rope_two_maps(b, c64_ref[...], s64_ref[...])
        dq_ref[:, sl] = (a * dq_scale).astype(dq_ref.dtype)
        dk_ref[:, sl] = b.astype(dk_ref.dtype)
    dv_ref[...] = proj(OFF_DV, DF_W).astype(dv_ref.dtype)

    g = proj(OFF_MG, LANES) + gb_ref[...]
    lane = lax.broadcasted_iota(jnp.int32, g.shape, 1)
    gates = jnp.where((lane & 4) != 0, _log_sigmoid(g), g)
    gt_ref[...] = gates
    for j in range(gates.shape[0] // CHUNK):
        gT_ref[j] = gates[j * CHUNK:(j + 1) * CHUNK, :].T


def _in_proj(layer, x, mod_l, gains, w_in_p, qk_gain_l, gate_bias_row, rope, latent):
    t = x.shape[0]
    tm = IN_TM
    n_tiles = t // tm
    per_seq = DEC_SEQ // tm
    if latent:
        cond_of = lambda i: 1 + i // per_seq
    else:
        cond_of = lambda i: 0
    row = lambda width: pl.BlockSpec((tm, width), lambda i: (i, 0))
    in_specs = [
        row(D_MODEL),
        pl.BlockSpec((None, 6, D_MODEL), lambda i: (cond_of(i), 0, 0)),
        pl.BlockSpec((4, D_MODEL), lambda i: (0, 0)),
        pl.BlockSpec((None, D_MODEL, IN_W_PACKED), lambda i: (layer, 0, 0), pipeline_mode=pl.Buffered(1)),
        pl.BlockSpec((2, HEAD_DIM), lambda i: (0, 0)),
        pl.BlockSpec((1, LANES), lambda i: (0, 0)),
    ]
    args = [x, mod_l, gains, w_in_p, qk_gain_l, gate_bias_row]
    if latent:
        in_specs += [pl.BlockSpec((tm, LANES), lambda i: (i % per_seq, 0))] * 4
        args += list(rope)
    kv_dt = BF16 if latent else F32
    widths_dtypes = [(ATT_W, BF16), (KV_W, kv_dt), (KV_W, kv_dt),
                     (ML_W, BF16), (ML_W, BF16), (ML_W, BF16), (ML_W, BF16),
                     (DF_W, BF16), (DF_W, kv_dt), (DF_W, kv_dt), (LANES, F32)]
    return pl.pallas_call(
        functools.partial(_in_proj_kernel, latent),
        grid=(n_tiles,),
        in_specs=in_specs,
        out_specs=[row(w) for w, _ in widths_dtypes]
        + [pl.BlockSpec((tm // CHUNK, LANES, CHUNK), lambda i: (i, 0, 0))],
        out_shape=[jax.ShapeDtypeStruct((t, w), dt) for w, dt in widths_dtypes]
        + [jax.ShapeDtypeStruct((t // CHUNK, LANES, CHUNK), F32)],
        compiler_params=_params(1),
        name="in_proj_lat" if latent else "in_proj_ctx",
    )(*args)


ATT_SUB = 128


def _with_ones(v):
    return jnp.concatenate([v, jnp.ones_like(v)], axis=1)


def _attend(q, k, va):
    s = _dot_nt(q, k)
    e = jnp.exp2(s - jnp.max(s, axis=-1, keepdims=True))
    oa = _dot(e.astype(BF16), va)
    return oa[:, :HEAD_DIM] / oa[:, HEAD_DIM:]


def _diff_lambda(lam_ref, lam_init):
    lv = lam_ref[...]
    a = jnp.sum(lv[0:1, :] * lv[1:2, :], axis=-1, keepdims=True)
    b = jnp.sum(lv[2:3, :] * lv[3:4, :], axis=-1, keepdims=True)
    return jnp.exp(a) - jnp.exp(b) + lam_init


def _diff_head(q, k, va, lam, gain, lam_init):
    lane = lax.broadcasted_iota(jnp.int32, q.shape, 1)
    zero = jnp.zeros_like(q)
    o1 = _attend(jnp.where(lane < DIFF_QK_DIM, q, zero), k, va)
    o2 = _attend(jnp.where(lane >= DIFF_QK_DIM, q, zero), k, va)
    return _rms(o1 - lam * o2, gain) * (1.0 - lam_init)


def _tri(lower):
    r = lax.broadcasted_iota(jnp.int32, (CHUNK, CHUNK), 0)
    c = lax.broadcasted_iota(jnp.int32, (CHUNK, CHUNK), 1)
    return (c <= r) if lower else (c >= r)


def _scan_masks():
    masks = (_tri(True), _tri(False))
    return masks, tuple(jnp.where(mk, 1.0, 0.0).astype(BF16) for mk in masks)


def _split3(g):
    g1 = g.astype(BF16)
    r1 = g - g1.astype(F32)
    g2 = r1.astype(BF16)
    g3 = (r1 - g2.astype(F32)).astype(BF16)
    return g1, g2, g3


def _gate_cumsums(g_parts, gT_parts, tris, d):
    b = sum(_dot(tris[d], p) for p in g_parts)
    bT = sum(_dot(p, tris[1 - d]) for p in gT_parts)
    return b, bT


def _mlstm_chunk(q, k, va, bcol, brow, icol, irow, mask, last, CN, m):
    logw = jnp.where(mask, bcol - brow + irow, -jnp.inf)
    inter = bcol + m
    m_row = jnp.maximum(inter, jnp.max(logw, axis=1, keepdims=True))
    sc = _dot_nt(q, k) * jnp.exp(logw - m_row)
    a_int = jnp.exp(inter - m_row)
    nd = _dot(sc.astype(BF16), va) + a_int * _dot(q, CN.astype(BF16))
    h = nd[:, :HEAD_DIM] / jnp.maximum(jnp.abs(nd[:, HEAD_DIM:]), jnp.exp(-m_row))
    m_new = m_row[last:last + 1, :]
    btot = bcol[last:last + 1, :]
    w_end = jnp.exp(btot - bcol + icol - m_new)
    decay = jnp.exp(btot + m - m_new)
    kw = (k.astype(F32) * w_end).astype(BF16)
    CN_new = decay * CN + _dot_tn(kw, va)
    return h, CN_new, m_new


def _gate_cols(d, h):
    return d * 8 + h, d * 8 + 4 + h


def _state_in(C, n_row):
    n_col = jnp.broadcast_to(n_row, (HEAD_DIM, HEAD_DIM)).T
    return jnp.concatenate([C, n_col], axis=1)


def _ctx_mix_kernel(lam_init, q_ref, k_ref, v_ref, mq_ref, mk_ref, mv_ref, og_ref, dq_ref, dk_ref,
                    dv_ref, gt_ref, gT_ref, mlg_ref, dfg_ref, lam_ref,
                    att_ref, ml_ref, dif_ref, C_ref, n_ref, m_ref):
    n_tok = SEQ
    rep = ATT_HEADS // ATT_KV_HEADS
    for g in range(ATT_KV_HEADS):
        sl = slice(g * HEAD_DIM, (g + 1) * HEAD_DIM)
        kb = k_ref[:, sl].astype(BF16)
        va = _with_ones(v_ref[:, sl].astype(BF16))
        for r in range(rep):
            hs = slice((g * rep + r) * HEAD_DIM, (g * rep + r + 1) * HEAD_DIM)
            att_ref[:, hs] = _attend(q_ref[:, hs], kb, va).astype(att_ref.dtype)

    lam = _diff_lambda(lam_ref, lam_init)
    for h in range(DIFF_HEADS):
        sl = slice(h * HEAD_DIM, (h + 1) * HEAD_DIM)
        d = _diff_head(dq_ref[:, sl], dk_ref[:, sl].astype(BF16), _with_ones(dv_ref[:, sl].astype(BF16)),
                       lam, dfg_ref[...], lam_init)
        dif_ref[:, sl] = d.astype(dif_ref.dtype)

    n_chunks = n_tok // CHUNK
    masks, tris = _scan_masks()
    forms = []
    for c in range(n_chunks):
        g = gt_ref[c * CHUNK:(c + 1) * CHUNK, :]
        gT = gT_ref[c]
        gp, gTp = _split3(g), _split3(gT)
        forms.append((g, gT) + tuple(_gate_cumsums(gp, gTp, tris, d) for d in range(2)))
    for h in range(MLSTM_HEADS):
        sl = slice(h * HEAD_DIM, (h + 1) * HEAD_DIM)
        hsum = [None] * n_chunks
        for d in range(2):
            ci, cf = _gate_cols(d, h)
            CN = jnp.zeros((HEAD_DIM, 2 * HEAD_DIM), F32)
            m = jnp.full((1, 1), -jnp.inf, F32)
            order = range(n_chunks) if d == 0 else range(n_chunks - 1, -1, -1)
            for c in order:
                g, gT = forms[c][0], forms[c][1]
                b, bT = forms[c][2 + d]
                rows = slice(c * CHUNK, (c + 1) * CHUNK)
                hc, CN, m = _mlstm_chunk(
                    mq_ref[rows, sl], mk_ref[rows, sl], _with_ones(mv_ref[rows, sl]),
                    b[:, cf:cf + 1], bT[cf:cf + 1, :], g[:, ci:ci + 1], gT[ci:ci + 1, :],
                    masks[d], CHUNK - 1 if d == 0 else 0, CN, m)
                hsum[c] = hc if hsum[c] is None else hsum[c] + hc
            o = d * MLSTM_HEADS + h
            C_ref[d, h] = CN[:, :HEAD_DIM]
            n_ref[o:o + 1, :] = CN[:, HEAD_DIM:].T[0:1, :]
            m_ref[o:o + 1, :] = jnp.broadcast_to(m, (1, LANES))
        for c in range(n_chunks):
            rows = slice(c * CHUNK, (c + 1) * CHUNK)
            y = _rms(hsum[c], mlg_ref[h:h + 1, :]) * og_ref[rows, sl].astype(F32)
            ml_ref[rows, sl] = y.astype(ml_ref.dtype)


def _ctx_mix(layer, q, k, v, mq, mk, mv, og, dq, dk, dv, gt, gT, ml_gain, diff_gain, diff_lambda):
    lam_init = 0.8 - 0.6 * math.exp(-0.3 * layer)
    row = lambda width: pl.BlockSpec((SEQ, width), lambda b: (b, 0))
    full = lambda a: pl.BlockSpec(a.shape, lambda b: (0,) * a.ndim)
    nh = 2 * MLSTM_HEADS
    return pl.pallas_call(
        functools.partial(_ctx_mix_kernel, lam_init),
        grid=(BATCH,),
        in_specs=[row(ATT_W), row(KV_W), row(KV_W), row(ML_W), row(ML_W), row(ML_W), row(ML_W),
                  row(DF_W), row(DF_W), row(DF_W), row(LANES),
                  pl.BlockSpec((SEQ // CHUNK, LANES, CHUNK), lambda b: (b, 0, 0)),
                  full(ml_gain), full(diff_gain), full(diff_lambda)],
        out_specs=[row(ATT_W), row(ML_W), row(DF_W),
                   pl.BlockSpec((None, 2, MLSTM_HEADS, HEAD_DIM, HEAD_DIM), lambda b: (b, 0, 0, 0, 0)),
                   pl.BlockSpec((None, nh, HEAD_DIM), lambda b: (b, 0, 0)),
                   pl.BlockSpec((None, nh, LANES), lambda b: (b, 0, 0))],
        out_shape=[jax.ShapeDtypeStruct((T_CTX, ATT_W), BF16),
                   jax.ShapeDtypeStruct((T_CTX, ML_W), BF16),
                   jax.ShapeDtypeStruct((T_CTX, DF_W), BF16),
                   jax.ShapeDtypeStruct((BATCH, 2, MLSTM_HEADS, HEAD_DIM, HEAD_DIM), F32),
                   jax.ShapeDtypeStruct((BATCH, nh, HEAD_DIM), F32),
                   jax.ShapeDtypeStruct((BATCH, nh, LANES), F32)],
        compiler_params=_params(1),
        name="ctx_mix",
    )(q, k, v, mq, mk, mv, og, dq, dk, dv, gt, gT, ml_gain, diff_gain, diff_lambda)


GQA_TQ = 512
DIFF_TQ = 1024


def _lat_gqa_kernel(q_ref, k_ref, v_ref, o_ref):
    k = k_ref[...]
    va = _with_ones(v_ref[...])
    for r in range(ATT_HEADS // ATT_KV_HEADS):
        cols = slice(r * HEAD_DIM, (r + 1) * HEAD_DIM)
        for j in range(q_ref.shape[0] // ATT_SUB):
            rows = slice(j * ATT_SUB, (j + 1) * ATT_SUB)
            o_ref[rows, cols] = _attend(q_ref[rows, cols], k, va).astype(o_ref.dtype)


def _lat_gqa(q, k_full, v_full):
    rep = ATT_HEADS // ATT_KV_HEADS
    nq = DEC_SEQ // GQA_TQ
    qspec = pl.BlockSpec((GQA_TQ, rep * HEAD_DIM), lambda b, g, i: (b * nq + i, g))
    kvspec = pl.BlockSpec((None, KEYS_LAT, HEAD_DIM), lambda b, g, i: (b, 0, g))
    return pl.pallas_call(
        _lat_gqa_kernel,
        grid=(DEC_BATCH, ATT_KV_HEADS, nq),
        in_specs=[qspec, kvspec, kvspec],
        out_specs=qspec,
        out_shape=jax.ShapeDtypeStruct((T_LAT, ATT_W), BF16),
        compiler_params=_params(3),
        name="lat_gqa",
    )(q, k_full, v_full)


def _lat_diff_kernel(lam_init, q_ref, k_ref, v_ref, dfg_ref, lam_ref, o_ref):
    lam = _diff_lambda(lam_ref, lam_init)
    k = k_ref[...]
    va = _with_ones(v_ref[...])
    for j in range(q_ref.shape[0] // ATT_SUB):
        rows = slice(j * ATT_SUB, (j + 1) * ATT_SUB)
        d = _diff_head(q_ref[rows, :], k, va, lam, dfg_ref[...], lam_init)
        o_ref[rows, :] = d.astype(o_ref.dtype)


def _lat_diff(layer, dq, dk_full, dv_full, diff_gain, diff_lambda):
    lam_init = 0.8 - 0.6 * math.exp(-0.3 * layer)
    nq = DEC_SEQ // DIFF_TQ
    qspec = pl.BlockSpec((DIFF_TQ, HEAD_DIM), lambda b, h, i: (b * nq + i, h))
    kvspec = pl.BlockSpec((None, KEYS_LAT, HEAD_DIM), lambda b, h, i: (b, 0, h))
    full = lambda a: pl.BlockSpec(a.shape, lambda b, h, i: (0,) * a.ndim)
    return pl.pallas_call(
        functools.partial(_lat_diff_kernel, lam_init),
        grid=(DEC_BATCH, DIFF_HEADS, nq),
        in_specs=[qspec, kvspec, kvspec, full(diff_gain), full(diff_lambda)],
        out_specs=qspec,
        out_shape=jax.ShapeDtypeStruct((T_LAT, DF_W), BF16),
        compiler_params=_params(3),
        name="lat_diff",
    )(dq, dk_full, dv_full, diff_gain, diff_lambda)


def _lat_mlstm_kernel(q_ref, k_ref, v_ref, og_ref, gt_ref, gT_ref, C0_ref, n0_ref, m0_ref, mlg_ref,
                      o_ref, hs_ref):
    n_chunks = DEC_SEQ // CHUNK
    masks, tris = _scan_masks()

    def step(second_half, t, carry):
        out = []
        for d in range(2):
            c = t if d == 0 else n_chunks - 1 - t
            rows = pl.ds(pl.multiple_of(c * CHUNK, CHUNK), CHUNK)
            g = gt_ref[rows, :]
            gT = gT_ref[c]
            b, bT = _gate_cumsums(_split3(g), _split3(gT), tris, d)
            for h in range(MLSTM_HEADS):
                sl = slice(h * HEAD_DIM, (h + 1) * HEAD_DIM)
                ci, cf = _gate_cols(d, h)
                CN, m = carry[d * MLSTM_HEADS + h]
                hc, CN, m = _mlstm_chunk(
                    q_ref[rows, sl], k_ref[rows, sl], _with_ones(v_ref[rows, sl]),
                    b[:, cf:cf + 1], bT[cf:cf + 1, :], g[:, ci:ci + 1], gT[ci:ci + 1, :],
                    masks[d], CHUNK - 1 if d == 0 else 0, CN, m)
                if second_half:
                    y = _rms(hs_ref[rows, sl] + hc, mlg_ref[h:h + 1, :]) * og_ref[rows, sl].astype(F32)
                    o_ref[rows, sl] = y.astype(o_ref.dtype)
                else:
                    hs_ref[rows, sl] = hc
                out.append((CN, m))
        return tuple(out)

    init = tuple((_state_in(C0_ref[d, h], n0_ref[d, h]), m0_ref[d, h][:, 0:1])
                 for d in range(2) for h in range(MLSTM_HEADS))
    half = n_chunks // 2
    carry = lax.fori_loop(0, half, functools.partial(step, False), init)
    lax.fori_loop(half, n_chunks, functools.partial(step, True), carry)


def _lat_mlstm(layer, mq, mk, mv, og, gt, gT, state_C, state_n6, state_m6, ml_gain):
    once = pl.Buffered(1)
    hspec = pl.BlockSpec((DEC_SEQ, ML_W), lambda b: (b, 0), pipeline_mode=once)
    st = lambda last2: pl.BlockSpec((None, None, 2, MLSTM_HEADS) + last2, lambda b: (b, layer, 0, 0, 0, 0))
    return pl.pallas_call(
        _lat_mlstm_kernel,
        grid=(DEC_BATCH,),
        in_specs=[hspec, hspec, hspec, hspec,
                  pl.BlockSpec((DEC_SEQ, LANES), lambda b: (b, 0), pipeline_mode=once),
                  pl.BlockSpec((DEC_SEQ // CHUNK, LANES, CHUNK), lambda b: (b, 0, 0), pipeline_mode=once),
                  st((HEAD_DIM, HEAD_DIM)), st((1, HEAD_DIM)), st((1, LANES)),
                  pl.BlockSpec(ml_gain.shape, lambda b: (0, 0))],
        out_specs=pl.BlockSpec((DEC_SEQ, ML_W), lambda b: (b, 0)),
        out_shape=jax.ShapeDtypeStruct((T_LAT, ML_W), BF16),
        scratch_shapes=[pltpu.VMEM((DEC_SEQ, ML_W), F32)],
        compiler_params=_params(1),
        name="lat_mlstm",
    )(mq, mk, mv, og, gt, gT, state_C, state_n6, state_m6, ml_gain)


OUT_TM = 512
FFN_TM = 512
FFN_TF = 1024


def _cond_index(latent, tm):
    per_seq = DEC_SEQ // tm
    if latent:
        return lambda i: 1 + i // per_seq
    return lambda i: 0


def _out_proj_kernel(att_ref, ml_ref, dif_ref, w_ref, x_ref, mod_ref, g_ref, o_ref):
    mix = (_dot(att_ref[...], w_ref[0:ATT_W, :])
           + _dot(ml_ref[...], w_ref[ATT_W:ATT_W + ML_W, :])
           + _dot(dif_ref[...], w_ref[ATT_W + ML_W:, :]))
    o_ref[...] = x_ref[...] + mod_ref[2:3, :] * _rms(mix, g_ref[1:2, :])


def _out_proj(layer, att, ml, dif, w_out_b, x, mod_l, gains, latent):
    t = x.shape[0]
    tm = OUT_TM
    cond_of = _cond_index(latent, tm)
    row = lambda width: pl.BlockSpec((tm, width), lambda i: (i, 0))
    return pl.pallas_call(
        _out_proj_kernel,
        grid=(t // tm,),
        in_specs=[row(ATT_W), row(ML_W), row(DF_W),
                  pl.BlockSpec((None, D_MODEL, D_MODEL), lambda i: (layer, 0, 0), pipeline_mode=pl.Buffered(1)),
                  row(D_MODEL),
                  pl.BlockSpec((None, 6, D_MODEL), lambda i: (cond_of(i), 0, 0)),
                  pl.BlockSpec((4, D_MODEL), lambda i: (0, 0))],
        out_specs=row(D_MODEL),
        out_shape=jax.ShapeDtypeStruct((t, D_MODEL), F32),
        compiler_params=_params(1),
        name="out_proj",
    )(att, ml, dif, w_out_b, x, mod_l, gains)


def _ffn_kernel(x_ref, mod_ref, g_ref, w1_ref, w2_ref, o_ref, hn_ref):
    f = pl.program_id(1)

    @pl.when(f == 0)
    def _():
        hn = _rms(x_ref[...], g_ref[2:3, :]) * (1.0 + mod_ref[4:5, :]) + mod_ref[3:4, :]
        hn_ref[...] = hn.astype(hn_ref.dtype)

    h = jnp.maximum(_dot(hn_ref[...], w1_ref[...]), 0.0)
    part = _dot((h * h).astype(BF16), w2_ref[...])

    @pl.when(f == 0)
    def _():
        o_ref[...] = part

    @pl.when(f > 0)
    def _():
        o_ref[...] += part

    @pl.when(f == pl.num_programs(1) - 1)
    def _():
        o_ref[...] = x_ref[...] + mod_ref[5:6, :] * _rms(o_ref[...], g_ref[3:4, :])


def _ffn(layer, x, mod_l, gains, w1_b, w2_b, latent):
    t = x.shape[0]
    tm, tf = FFN_TM, FFN_TF
    cond_of = _cond_index(latent, tm)
    return pl.pallas_call(
        _ffn_kernel,
        grid=(t // tm, D_FF // tf),
        in_specs=[pl.BlockSpec((tm, D_MODEL), lambda i, f: (i, 0)),
                  pl.BlockSpec((None, 6, D_MODEL), lambda i, f: (cond_of(i), 0, 0)),
                  pl.BlockSpec((4, D_MODEL), lambda i, f: (0, 0)),
                  pl.BlockSpec((None, D_MODEL, tf), lambda i, f: (layer, 0, f)),
                  pl.BlockSpec((None, tf, D_MODEL), lambda i, f: (layer, f, 0))],
        out_specs=pl.BlockSpec((tm, D_MODEL), lambda i, f: (i, 0)),
        out_shape=jax.ShapeDtypeStruct((t, D_MODEL), F32),
        scratch_shapes=[pltpu.VMEM((tm, D_MODEL), BF16)],
        compiler_params=_params(2),
        name="ffn",
    )(x, mod_l, gains, w1_b, w2_b)


def _rope_tables():
    def angles(dim):
        rows = DEC_SEQ // GRID_W
        row_idx = jnp.repeat(jnp.arange(rows), GRID_W).astype(F32)
        col_idx = jnp.tile(jnp.arange(GRID_W), rows).astype(F32)
        n_freq = dim // 4
        inv = ROPE_THETA ** (-jnp.arange(n_freq, dtype=F32) / n_freq)
        ang = jnp.concatenate([row_idx[:, None] * inv, col_idx[:, None] * inv], axis=-1)
        return jnp.cos(ang), jnp.sin(ang)

    c, s = angles(HEAD_DIM)
    c2, s2 = angles(DIFF_QK_DIM)
    c128 = jnp.concatenate([c, c], axis=-1)
    s128 = jnp.concatenate([-s, s], axis=-1)
    c64 = jnp.concatenate([c2, c2, c2, c2], axis=-1)
    s64 = jnp.concatenate([-s2, s2, -s2, s2], axis=-1)
    return c128, s128, c64, s64


def _pack_w_in(w_in):
    g0 = OFF_MO + ML_W
    pad = jnp.zeros(w_in.shape[:2] + (LANES - N_GATES,), w_in.dtype)
    packed = jnp.concatenate([w_in[..., :g0], w_in[..., g0 + N_GATES:], w_in[..., g0:g0 + N_GATES], pad], axis=-1)
    return packed.astype(BF16)


def kernel(x_prompt, x_sample, c, cache_gqa_k, cache_gqa_v, cache_diff_k, cache_diff_v, state_mlstm_C,
           state_mlstm_n, state_mlstm_m, c_ctx, w_ada, b_ada, norm_gain, w_in, w_out, qk_gain,
           mlstm_gate_bias, mlstm_head_gain, diff_lambda, diff_head_gain, w_ff1, w_ff2):
    cond3 = jnp.concatenate([c_ctx[None, :], c], axis=0)
    mod = _modulation(cond3, w_ada, b_ada)

    w_in_p = _pack_w_in(w_in)
    w_out_b = w_out.astype(BF16)
    w1_b = w_ff1.astype(BF16)
    w2_b = w_ff2.astype(BF16)
    gate_bias = jnp.pad(mlstm_gate_bias.reshape(DEPTH, 1, N_GATES), ((0, 0), (0, 0), (0, LANES - N_GATES)))
    diff_gain = diff_head_gain.reshape(DEPTH, 1, HEAD_DIM)
    rope = _rope_tables()

    state_n6 = state_mlstm_n.reshape(DEC_BATCH, DEPTH, 2, MLSTM_HEADS, 1, HEAD_DIM)
    state_m6 = jnp.broadcast_to(state_mlstm_m[..., None, None], (DEC_BATCH, DEPTH, 2, MLSTM_HEADS, 1, LANES))

    xp = x_prompt.reshape(T_CTX, D_MODEL)
    xs = x_sample.reshape(T_LAT, D_MODEL)
    new = [[] for _ in range(7)]

    for l in range(DEPTH):
        (q, k, v, mq, mk, mv, og, dq, dk, dv, gt, gT) = _in_proj(
            l, xp, mod[l], norm_gain[l], w_in_p, qk_gain[l], gate_bias[l], None, False)
        att, ml, dif, C, n, m = _ctx_mix(l, q, k, v, mq, mk, mv, og, dq, dk, dv, gt, gT,
                                         mlstm_head_gain[l], diff_gain[l], diff_lambda[l])
        xp = _out_proj(l, att, ml, dif, w_out_b, xp, mod[l], norm_gain[l], False)
        xp = _ffn(l, xp, mod[l], norm_gain[l], w1_b, w2_b, False)
        new[0].append(k.reshape(BATCH, SEQ, ATT_KV_HEADS, HEAD_DIM))
        new[1].append(v.reshape(BATCH, SEQ, ATT_KV_HEADS, HEAD_DIM))
        new[2].append(dk.reshape(BATCH, SEQ, DIFF_HEADS, HEAD_DIM))
        new[3].append(dv.reshape(BATCH, SEQ, DIFF_HEADS, HEAD_DIM))
        new[4].append(C)
        new[5].append(n.reshape(BATCH, 2, MLSTM_HEADS, HEAD_DIM))
        new[6].append(m[:, :, 0].reshape(BATCH, 2, MLSTM_HEADS))

        (q, k, v, mq, mk, mv, og, dq, dk, dv, gt, gT) = _in_proj(
            l, xs, mod[l], norm_gain[l], w_in_p, qk_gain[l], gate_bias[l], rope, True)
        as_keys = lambda new_rows, cache, width: jnp.concatenate(
            [new_rows.reshape(DEC_BATCH, DEC_SEQ, width),
             cache[:, l].reshape(DEC_BATCH, PAST_LEN, width).astype(BF16)], axis=1)
        att = _lat_gqa(q, as_keys(k, cache_gqa_k, KV_W), as_keys(v, cache_gqa_v, KV_W))
        dif = _lat_diff(l, dq, as_keys(dk, cache_diff_k, DF_W), as_keys(dv, cache_diff_v, DF_W),
                        diff_gain[l], diff_lambda[l])
        ml = _lat_mlstm(l, mq, mk, mv, og, gt, gT, state_mlstm_C, state_n6, state_m6, mlstm_head_gain[l])
        xs = _out_proj(l, att, ml, dif, w_out_b, xs, mod[l], norm_gain[l], True)
        xs = _ffn(l, xs, mod[l], norm_gain[l], w1_b, w2_b, True)

    outs = [jnp.stack(lst, axis=1) for lst in new]
    return (xp.reshape(BATCH, SEQ, D_MODEL), xs.reshape(DEC_BATCH, DEC_SEQ, D_MODEL), *outs)
```

```python
import functools
import math

import jax
import jax.numpy as jnp
from jax import lax
from jax.experimental import pallas as pl
from jax.experimental.pallas import tpu as pltpu

F32 = jnp.float32
BF16 = jnp.bfloat16

D_MODEL = 2048
BATCH = 32
SEQ = 256
DEPTH = 4
DEC_BATCH = 2
DEC_SEQ = 4096
PAST_LEN = 256
GRID_W = 64
HEAD_DIM = 128
ATT_W = 1024
ML_W = 512
DF_W = 512
ATT_HEADS = 8
ATT_KV_HEADS = 2
KV_W = 256
MLSTM_HEADS = 4
DIFF_HEADS = 4
DIFF_QK_DIM = 64
N_GATES = 16
D_FF = 8192
CHUNK = 128
ROPE_THETA = 10000.0
EPS = 1e-6
LOG2E = 1.4426950408889634
LANES = 128

T_CTX = BATCH * SEQ
T_LAT = DEC_BATCH * DEC_SEQ
KEYS_LAT = DEC_SEQ + PAST_LEN

OFF_AQ, OFF_AK, OFF_AV = 0, 1024, 1280
OFF_MQ, OFF_MK, OFF_MV, OFF_MO = 1536, 2048, 2560, 3072
OFF_DQ, OFF_DK, OFF_DV = 3584, 4096, 4608
OFF_MG = 5120
IN_W_PACKED = OFF_MG + LANES

VMEM_LIMIT = 52 * 1024 * 1024


def _dot(a, b):
    return jnp.dot(a, b, preferred_element_type=F32)


def _dot_nt(a, b):
    return lax.dot_general(a, b, (((1,), (1,)), ((), ())), preferred_element_type=F32)


def _dot_tn(a, b):
    return lax.dot_general(a, b, (((0,), (0,)), ((), ())), preferred_element_type=F32)


def _rms(x, gain):
    ms = jnp.mean(x * x, axis=-1, keepdims=True)
    return x * lax.rsqrt(ms + EPS) * gain


def _sigmoid(x):
    return 1.0 / (1.0 + jnp.exp(-x))


def _log_sigmoid(x):
    return jnp.minimum(x, 0.0) - jnp.log(1.0 + jnp.exp(-jnp.abs(x)))


def _params(n_axes):
    return pltpu.CompilerParams(dimension_semantics=("arbitrary",) * n_axes,
                                vmem_limit_bytes=VMEM_LIMIT)


MOD_TN = 512
MOD_KC = 64


def _mod_kernel(c_ref, w_ref, b_ref, o_ref, s_ref):
    @pl.when((pl.program_id(0) == 0) & (pl.program_id(1) == 0))
    def _():
        cb = c_ref[...]
        s_ref[...] = cb * _sigmoid(cb)

    tn = w_ref.shape[1]
    reps = tn // LANES

    def body(g, accs):
        rows = pl.ds(pl.multiple_of(g * MOD_KC, MOD_KC), MOD_KC)
        wk = w_ref[rows, :].reshape(MOD_KC // 8, 8, tn)
        out = []
        for r in range(3):
            sk = s_ref[r, rows, :].reshape(MOD_KC // 8, 8, LANES)
            out.append(accs[r] + jnp.sum(wk * jnp.concatenate([sk] * reps, axis=2), axis=0))
        return tuple(out)

    zero = jnp.zeros((8, tn), F32)
    accs = lax.fori_loop(0, D_MODEL // MOD_KC, body, (zero, zero, zero))
    rows = [jnp.sum(a, axis=0, keepdims=True) + b_ref[...] for a in accs]
    rows.append(jnp.zeros((5, tn), F32))
    o_ref[...] = jnp.concatenate(rows, axis=0)


def _modulation(cond3, w_ada, b_ada):
    cb = jnp.broadcast_to(cond3[:, :, None], (3, D_MODEL, LANES))
    n = 6 * D_MODEL
    out = pl.pallas_call(
        _mod_kernel,
        grid=(DEPTH, n // MOD_TN),
        in_specs=[
            pl.BlockSpec((3, D_MODEL, LANES), lambda l, j: (0, 0, 0)),
            pl.BlockSpec((None, D_MODEL, MOD_TN), lambda l, j: (l, 0, j)),
            pl.BlockSpec((None, 1, MOD_TN), lambda l, j: (l, 0, j)),
        ],
        out_specs=pl.BlockSpec((None, 8, MOD_TN), lambda l, j: (l, 0, j)),
        out_shape=jax.ShapeDtypeStruct((DEPTH, 8, n), F32),
        scratch_shapes=[pltpu.VMEM((3, D_MODEL, LANES), F32)],
        compiler_params=_params(2),
        name="adaln_mod",
    )(cb, w_ada, b_ada.reshape(DEPTH, 1, n))
    return out[:, :3].reshape(DEPTH, 3, 6, D_MODEL)


IN_TM = 256


def _rope_full(a, c, s):
    return a * c + pltpu.roll(a, 64, 1) * s


def _rope_two_maps(a, c, s):
    lane = lax.broadcasted_iota(jnp.int32, a.shape, 1)
    partner = jnp.where((lane % 64) < 32, pltpu.roll(a, 96, 1), pltpu.roll(a, 32, 1))
    return a * c + partner * s


def _in_proj_kernel(latent, *refs):
    if latent:
        (x_ref, mod_ref, g_ref, w_ref, qkg_ref, gb_ref, c128_ref, s128_ref, c64_ref, s64_ref,
         q_ref, k_ref, v_ref, mq_ref, mk_ref, mv_ref, og_ref, dq_ref, dk_ref, dv_ref, gt_ref, gT_ref) = refs
    else:
        (x_ref, mod_ref, g_ref, w_ref, qkg_ref, gb_ref,
         q_ref, k_ref, v_ref, mq_ref, mk_ref, mv_ref, og_ref, dq_ref, dk_ref, dv_ref, gt_ref, gT_ref) = refs

    x = x_ref[...]
    hn = _rms(x, g_ref[0:1, :]) * (1.0 + mod_ref[1:2, :]) + mod_ref[0:1, :]
    hnb = hn.astype(BF16)

    def proj(off, width):
        return _dot(hnb, w_ref[:, off:off + width])

    q_scale = HEAD_DIM ** -0.5 * LOG2E
    aq = proj(OFF_AQ, ATT_W)
    for h in range(ATT_HEADS):
        a = _rms(aq[:, h * HEAD_DIM:(h + 1) * HEAD_DIM], qkg_ref[0:1, :])
        if latent:
            a = _rope_full(a, c128_ref[...], s128_ref[...])
        q_ref[:, h * HEAD_DIM:(h + 1) * HEAD_DIM] = (a * q_scale).astype(q_ref.dtype)

    ak = proj(OFF_AK, KV_W)
    for h in range(ATT_KV_HEADS):
        a = _rms(ak[:, h * HEAD_DIM:(h + 1) * HEAD_DIM], qkg_ref[1:2, :])
        if latent:
            a = _rope_full(a, c128_ref[...], s128_ref[...])
        k_ref[:, h * HEAD_DIM:(h + 1) * HEAD_DIM] = a.astype(k_ref.dtype)

    v_ref[...] = proj(OFF_AV, KV_W).astype(v_ref.dtype)
    mq_ref[...] = proj(OFF_MQ, ML_W).astype(mq_ref.dtype)
    mk_ref[...] = (proj(OFF_MK, ML_W) * (HEAD_DIM ** -0.5)).astype(mk_ref.dtype)
    mv_ref[...] = proj(OFF_MV, ML_W).astype(mv_ref.dtype)
    og_ref[...] = _sigmoid(proj(OFF_MO, ML_W)).astype(og_ref.dtype)

    dq = proj(OFF_DQ, DF_W)
    dk = proj(OFF_DK, DF_W)
    dq_scale = DIFF_QK_DIM ** -0.5 * LOG2E
    for h in range(DIFF_HEADS):
        sl = slice(h * HEAD_DIM, (h + 1) * HEAD_DIM)
        a, b = dq[:, sl], dk[:, sl]
        if latent:
            a = _rope_two_maps(a, c64_ref[...], s64_ref[...])
            b = _rope_two_maps(b, c64_ref[...], s64_ref[...])
        dq_ref[:, sl] = (a * dq_scale).astype(dq_ref.dtype)
        dk_ref[:, sl] = b.astype(dk_ref.dtype)
    dv_ref[...] = proj(OFF_DV, DF_W).astype(dv_ref.dtype)

    g = proj(OFF_MG, LANES) + gb_ref[...]
    lane = lax.broadcasted_iota(jnp.int32, g.shape, 1)
    gates = jnp.where((lane & 4) != 0, _log_sigmoid(g), g)
    gt_ref[...] = gates
    for j in range(gates.shape[0] // CHUNK):
        gT_ref[j] = gates[j * CHUNK:(j + 1) * CHUNK, :].T


def _in_proj(layer, x, mod_l, gains, w_in_p, qk_gain_l, gate_bias_row, rope, latent):
    t = x.shape[0]
    tm = IN_TM
    n_tiles = t // tm
    per_seq = DEC_SEQ // tm
    if latent:
        cond_of = lambda i: 1 + i // per_seq
    else:
        cond_of = lambda i: 0
    row = lambda width: pl.BlockSpec((tm, width), lambda i: (i, 0))
    in_specs = [
        row(D_MODEL),
        pl.BlockSpec((None, 6, D_MODEL), lambda i: (cond_of(i), 0, 0)),
        pl.BlockSpec((4, D_MODEL), lambda i: (0, 0)),
        pl.BlockSpec((None, D_MODEL, IN_W_PACKED), lambda i: (layer, 0, 0), pipeline_mode=pl.Buffered(1)),
        pl.BlockSpec((2, HEAD_DIM), lambda i: (0, 0)),
        pl.BlockSpec((1, LANES), lambda i: (0, 0)),
    ]
    args = [x, mod_l, gains, w_in_p, qk_gain_l, gate_bias_row]
    if latent:
        in_specs += [pl.BlockSpec((tm, LANES), lambda i: (i % per_seq, 0))] * 4
        args += list(rope)
    kv_dt = BF16 if latent else F32
    widths_dtypes = [(ATT_W, BF16), (KV_W, kv_dt), (KV_W, kv_dt),
                     (ML_W, BF16), (ML_W, BF16), (ML_W, BF16), (ML_W, BF16),
                     (DF_W, BF16), (DF_W, kv_dt), (DF_W, kv_dt), (LANES, F32)]
    return pl.pallas_call(
        functools.partial(_in_proj_kernel, latent),
        grid=(n_tiles,),
        in_specs=in_specs,
        out_specs=[row(w) for w, _ in widths_dtypes]
        + [pl.BlockSpec((tm // CHUNK, LANES, CHUNK), lambda i: (i, 0, 0))],
        out_shape=[jax.ShapeDtypeStruct((t, w), dt) for w, dt in widths_dtypes]
        + [jax.ShapeDtypeStruct((t // CHUNK, LANES, CHUNK), F32)],
        compiler_params=_params(1),
        name="in_proj_lat" if latent else "in_proj_ctx",
    )(*args)


ATT_SUB = 128


def _with_ones(v):
    return jnp.concatenate([v, jnp.ones_like(v)], axis=1)


def _attend(q, k, va):
    s = _dot_nt(q, k)
    e = jnp.exp2(s - jnp.max(s, axis=-1, keepdims=True))
    oa = _dot(e.astype(BF16), va)
    return oa[:, :HEAD_DIM] / oa[:, HEAD_DIM:]


def _diff_lambda(lam_ref, lam_init):
    lv = lam_ref[...]
    a = jnp.sum(lv[0:1, :] * lv[1:2, :], axis=-1, keepdims=True)
    b = jnp.sum(lv[2:3, :] * lv[3:4, :], axis=-1, keepdims=True)
    return jnp.exp(a) - jnp.exp(b) + lam_init


def _diff_head(q, k, va, lam, gain, lam_init):
    lane = lax.broadcasted_iota(jnp.int32, q.shape, 1)
    zero = jnp.zeros_like(q)
    o1 = _attend(jnp.where(lane < DIFF_QK_DIM, q, zero), k, va)
    o2 = _attend(jnp.where(lane >= DIFF_QK_DIM, q, zero), k, va)
    return _rms(o1 - lam * o2, gain) * (1.0 - lam_init)


def _tri(lower):
    r = lax.broadcasted_iota(jnp.int32, (CHUNK, CHUNK), 0)
    c = lax.broadcasted_iota(jnp.int32, (CHUNK, CHUNK), 1)
    return (c <= r) if lower else (c >= r)


def _scan_masks():
    masks = (_tri(True), _tri(False))
    return masks, tuple(jnp.where(mk, 1.0, 0.0).astype(BF16) for mk in masks)


def _split3(g):
    g1 = g.astype(BF16)
    r1 = g - g1.astype(F32)
    g2 = r1.astype(BF16)
    g3 = (r1 - g2.astype(F32)).astype(BF16)
    return g1, g2, g3


def _gate_cumsums(g_parts, gT_parts, tris, d):
    b = sum(_dot(tris[d], p) for p in g_parts)
    bT = sum(_dot(p, tris[1 - d]) for p in gT_parts)
    return b, bT


def _mlstm_chunk(q, k, va, bcol, brow, icol, irow, mask, last, CN, m):
    bfull = jnp.broadcast_to(bcol, (CHUNK, CHUNK))
    ifull = jnp.broadcast_to(icol, (CHUNK, CHUNK))
    logw = jnp.where(mask, bfull - brow + irow, -jnp.inf)
    inter = bfull + m
    m_row = jnp.maximum(inter, jnp.max(logw, axis=1, keepdims=True))
    sc = _dot_nt(q, k) * jnp.exp(logw - m_row)
    a_int = jnp.exp(inter - m_row)
    qc = _dot(q, CN.astype(BF16))
    nd = _dot(sc.astype(BF16), va)
    num = nd[:, :HEAD_DIM] + a_int * qc[:, :HEAD_DIM]
    den = nd[:, HEAD_DIM:] + a_int * qc[:, HEAD_DIM:]
    h = num / jnp.maximum(jnp.abs(den), jnp.exp(-m_row))
    m_new = m_row[last:last + 1, :]
    btot = bfull[last:last + 1, :]
    w_end = jnp.exp(btot - bfull + ifull - m_new)
    decay = jnp.exp(btot + m - m_new)
    kw = (k.astype(F32) * w_end).astype(BF16)
    CN_new = jnp.concatenate([decay, decay], axis=1) * CN + _dot_tn(kw, va)
    return h, CN_new, m_new


def _gate_cols(d, h):
    return d * 8 + h, d * 8 + 4 + h


def _state_in(C, n_row):
    n_col = jnp.broadcast_to(n_row, (HEAD_DIM, HEAD_DIM)).T
    return jnp.concatenate([C, n_col], axis=1)


def _ctx_mix_kernel(lam_init, q_ref, k_ref, v_ref, mq_ref, mk_ref, mv_ref, og_ref, dq_ref, dk_ref,
                    dv_ref, gt_ref, gT_ref, mlg_ref, dfg_ref, lam_ref,
                    att_ref, ml_ref, dif_ref, C_ref, n_ref, m_ref):
    n_tok = SEQ
    rep = ATT_HEADS // ATT_KV_HEADS
    for g in range(ATT_KV_HEADS):
        sl = slice(g * HEAD_DIM, (g + 1) * HEAD_DIM)
        kb = k_ref[:, sl].astype(BF16)
        va = _with_ones(v_ref[:, sl].astype(BF16))
        for r in range(rep):
            hs = slice((g * rep + r) * HEAD_DIM, (g * rep + r + 1) * HEAD_DIM)
            att_ref[:, hs] = _attend(q_ref[:, hs], kb, va).astype(att_ref.dtype)

    lam = _diff_lambda(lam_ref, lam_init)
    for h in range(DIFF_HEADS):
        sl = slice(h * HEAD_DIM, (h + 1) * HEAD_DIM)
        d = _diff_head(dq_ref[:, sl], dk_ref[:, sl].astype(BF16), _with_ones(dv_ref[:, sl].astype(BF16)),
                       lam, dfg_ref[...], lam_init)
        dif_ref[:, sl] = d.astype(dif_ref.dtype)

    n_chunks = n_tok // CHUNK
    masks, tris = _scan_masks()
    forms = []
    for c in range(n_chunks):
        g = gt_ref[c * CHUNK:(c + 1) * CHUNK, :]
        gT = gT_ref[c]
        gp, gTp = _split3(g), _split3(gT)
        forms.append((g, gT) + tuple(_gate_cumsums(gp, gTp, tris, d) for d in range(2)))
    for h in range(MLSTM_HEADS):
        sl = slice(h * HEAD_DIM, (h + 1) * HEAD_DIM)
        hsum = [None] * n_chunks
        for d in range(2):
            ci, cf = _gate_cols(d, h)
            CN = jnp.zeros((HEAD_DIM, 2 * HEAD_DIM), F32)
            m = jnp.full((1, LANES), -jnp.inf, F32)
            order = range(n_chunks) if d == 0 else range(n_chunks - 1, -1, -1)
            for c in order:
                g, gT = forms[c][0], forms[c][1]
                b, bT = forms[c][2 + d]
                rows = slice(c * CHUNK, (c + 1) * CHUNK)
                hc, CN, m = _mlstm_chunk(
                    mq_ref[rows, sl], mk_ref[rows, sl], _with_ones(mv_ref[rows, sl]),
                    b[:, cf:cf + 1], bT[cf:cf + 1, :], g[:, ci:ci + 1], gT[ci:ci + 1, :],
                    masks[d], CHUNK - 1 if d == 0 else 0, CN, m)
                hsum[c] = hc if hsum[c] is None else hsum[c] + hc
            o = d * MLSTM_HEADS + h
            C_ref[d, h] = CN[:, :HEAD_DIM]
            n_ref[o:o + 1, :] = CN[:, HEAD_DIM:].T[0:1, :]
            m_ref[o:o + 1, :] = m
        for c in range(n_chunks):
            rows = slice(c * CHUNK, (c + 1) * CHUNK)
            y = _rms(hsum[c], mlg_ref[h:h + 1, :]) * og_ref[rows, sl].astype(F32)
            ml_ref[rows, sl] = y.astype(ml_ref.dtype)


def _ctx_mix(layer, q, k, v, mq, mk, mv, og, dq, dk, dv, gt, gT, ml_gain, diff_gain, diff_lambda):
    lam_init = 0.8 - 0.6 * math.exp(-0.3 * layer)
    row = lambda width: pl.BlockSpec((SEQ, width), lambda b: (b, 0))
    full = lambda a: pl.BlockSpec(a.shape, lambda b: (0,) * a.ndim)
    nh = 2 * MLSTM_HEADS
    return pl.pallas_call(
        functools.partial(_ctx_mix_kernel, lam_init),
        grid=(BATCH,),
        in_specs=[row(ATT_W), row(KV_W), row(KV_W), row(ML_W), row(ML_W), row(ML_W), row(ML_W),
                  row(DF_W), row(DF_W), row(DF_W), row(LANES),
                  pl.BlockSpec((SEQ // CHUNK, LANES, CHUNK), lambda b: (b, 0, 0)),
                  full(ml_gain), full(diff_gain), full(diff_lambda)],
        out_specs=[row(ATT_W), row(ML_W), row(DF_W),
                   pl.BlockSpec((None, 2, MLSTM_HEADS, HEAD_DIM, HEAD_DIM), lambda b: (b, 0, 0, 0, 0)),
                   pl.BlockSpec((None, nh, HEAD_DIM), lambda b: (b, 0, 0)),
                   pl.BlockSpec((None, nh, LANES), lambda b: (b, 0, 0))],
        out_shape=[jax.ShapeDtypeStruct((T_CTX, ATT_W), BF16),
                   jax.ShapeDtypeStruct((T_CTX, ML_W), BF16),
                   jax.ShapeDtypeStruct((T_CTX, DF_W), BF16),
                   jax.ShapeDtypeStruct((BATCH, 2, MLSTM_HEADS, HEAD_DIM, HEAD_DIM), F32),
                   jax.ShapeDtypeStruct((BATCH, nh, HEAD_DIM), F32),
                   jax.ShapeDtypeStruct((BATCH, nh, LANES), F32)],
        compiler_params=_params(1),
        name="ctx_mix",
    )(q, k, v, mq, mk, mv, og, dq, dk, dv, gt, gT, ml_gain, diff_gain, diff_lambda)


GQA_TQ = 512
DIFF_TQ = 1024


def _lat_gqa_kernel(q_ref, k_ref, v_ref, o_ref):
    k = k_ref[...]
    va = _with_ones(v_ref[...])
    for r in range(ATT_HEADS // ATT_KV_HEADS):
        cols = slice(r * HEAD_DIM, (r + 1) * HEAD_DIM)
        for j in range(q_ref.shape[0] // ATT_SUB):
            rows = slice(j * ATT_SUB, (j + 1) * ATT_SUB)
            o_ref[rows, cols] = _attend(q_ref[rows, cols], k, va).astype(o_ref.dtype)


def _lat_gqa(q, k_full, v_full):
    rep = ATT_HEADS // ATT_KV_HEADS
    nq = DEC_SEQ // GQA_TQ
    qspec = pl.BlockSpec((GQA_TQ, rep * HEAD_DIM), lambda b, g, i: (b * nq + i, g))
    kvspec = pl.BlockSpec((None, KEYS_LAT, HEAD_DIM), lambda b, g, i: (b, 0, g))
    return pl.pallas_call(
        _lat_gqa_kernel,
        grid=(DEC_BATCH, ATT_KV_HEADS, nq),
        in_specs=[qspec, kvspec, kvspec],
        out_specs=qspec,
        out_shape=jax.ShapeDtypeStruct((T_LAT, ATT_W), BF16),
        compiler_params=_params(3),
        name="lat_gqa",
    )(q, k_full, v_full)


def _lat_diff_kernel(lam_init, q_ref, k_ref, v_ref, dfg_ref, lam_ref, o_ref):
    lam = _diff_lambda(lam_ref, lam_init)
    k = k_ref[...]
    va = _with_ones(v_ref[...])
    for j in range(q_ref.shape[0] // ATT_SUB):
        rows = slice(j * ATT_SUB, (j + 1) * ATT_SUB)
        d = _diff_head(q_ref[rows, :], k, va, lam, dfg_ref[...], lam_init)
        o_ref[rows, :] = d.astype(o_ref.dtype)


def _lat_diff(layer, dq, dk_full, dv_full, diff_gain, diff_lambda):
    lam_init = 0.8 - 0.6 * math.exp(-0.3 * layer)
    nq = DEC_SEQ // DIFF_TQ
    qspec = pl.BlockSpec((DIFF_TQ, HEAD_DIM), lambda b, h, i: (b * nq + i, h))
    kvspec = pl.BlockSpec((None, KEYS_LAT, HEAD_DIM), lambda b, h, i: (b, 0, h))
    full = lambda a: pl.BlockSpec(a.shape, lambda b, h, i: (0,) * a.ndim)
    return pl.pallas_call(
        functools.partial(_lat_diff_kernel, lam_init),
        grid=(DEC_BATCH, DIFF_HEADS, nq),
        in_specs=[qspec, kvspec, kvspec, full(diff_gain), full(diff_lambda)],
        out_specs=qspec,
        out_shape=jax.ShapeDtypeStruct((T_LAT, DF_W), BF16),
        compiler_params=_params(3),
        name="lat_diff",
    )(dq, dk_full, dv_full, diff_gain, diff_lambda)


def _lat_mlstm_kernel(q_ref, k_ref, v_ref, og_ref, gt_ref, gT_ref, C0_ref, n0_ref, m0_ref, mlg_ref,
                      o_ref, hs_ref):
    n_chunks = DEC_SEQ // CHUNK
    masks, tris = _scan_masks()

    def step(second_half, t, carry):
        out = []
        for d in range(2):
            c = t if d == 0 else n_chunks - 1 - t
            rows = pl.ds(pl.multiple_of(c * CHUNK, CHUNK), CHUNK)
            g = gt_ref[rows, :]
            gT = gT_ref[c]
            b, bT = _gate_cumsums(_split3(g), _split3(gT), tris, d)
            for h in range(MLSTM_HEADS):
                sl = slice(h * HEAD_DIM, (h + 1) * HEAD_DIM)
                ci, cf = _gate_cols(d, h)
                CN, m = carry[d * MLSTM_HEADS + h]
                hc, CN, m = _mlstm_chunk(
                    q_ref[rows, sl], k_ref[rows, sl], _with_ones(v_ref[rows, sl]),
                    b[:, cf:cf + 1], bT[cf:cf + 1, :], g[:, ci:ci + 1], gT[ci:ci + 1, :],
                    masks[d], CHUNK - 1 if d == 0 else 0, CN, m)
                if second_half:
                    y = _rms(hs_ref[rows, sl] + hc, mlg_ref[h:h + 1, :]) * og_ref[rows, sl].astype(F32)
                    o_ref[rows, sl] = y.astype(o_ref.dtype)
                else:
                    hs_ref[rows, sl] = hc
                out.append((CN, m))
        return tuple(out)

    init = tuple((_state_in(C0_ref[d, h], n0_ref[d, h]), m0_ref[d, h])
                 for d in range(2) for h in range(MLSTM_HEADS))
    half = n_chunks // 2
    carry = lax.fori_loop(0, half, functools.partial(step, False), init)
    lax.fori_loop(half, n_chunks, functools.partial(step, True), carry)


def _lat_mlstm(layer, mq, mk, mv, og, gt, gT, state_C, state_n6, state_m6, ml_gain):
    once = pl.Buffered(1)
    hspec = pl.BlockSpec((DEC_SEQ, ML_W), lambda b: (b, 0), pipeline_mode=once)
    st = lambda last2: pl.BlockSpec((None, None, 2, MLSTM_HEADS) + last2, lambda b: (b, layer, 0, 0, 0, 0))
    return pl.pallas_call(
        _lat_mlstm_kernel,
        grid=(DEC_BATCH,),
        in_specs=[hspec, hspec, hspec, hspec,
                  pl.BlockSpec((DEC_SEQ, LANES), lambda b: (b, 0), pipeline_mode=once),
                  pl.BlockSpec((DEC_SEQ // CHUNK, LANES, CHUNK), lambda b: (b, 0, 0), pipeline_mode=once),
                  st((HEAD_DIM, HEAD_DIM)), st((1, HEAD_DIM)), st((1, LANES)),
                  pl.BlockSpec(ml_gain.shape, lambda b: (0, 0))],
        out_specs=pl.BlockSpec((DEC_SEQ, ML_W), lambda b: (b, 0)),
        out_shape=jax.ShapeDtypeStruct((T_LAT, ML_W), BF16),
        scratch_shapes=[pltpu.VMEM((DEC_SEQ, ML_W), F32)],
        compiler_params=_params(1),
        name="lat_mlstm",
    )(mq, mk, mv, og, gt, gT, state_C, state_n6, state_m6, ml_gain)


OUT_TM = 512
FFN_TM = 512
FFN_TF = 1024


def _cond_index(latent, tm):
    per_seq = DEC_SEQ // tm
    if latent:
        return lambda i: 1 + i // per_seq
    return lambda i: 0


def _out_proj_kernel(att_ref, ml_ref, dif_ref, w_ref, x_ref, mod_ref, g_ref, o_ref):
    mix = (_dot(att_ref[...], w_ref[0:ATT_W, :])
           + _dot(ml_ref[...], w_ref[ATT_W:ATT_W + ML_W, :])
           + _dot(dif_ref[...], w_ref[ATT_W + ML_W:, :]))
    o_ref[...] = x_ref[...] + mod_ref[2:3, :] * _rms(mix, g_ref[1:2, :])


def _out_proj(layer, att, ml, dif, w_out_b, x, mod_l, gains, latent):
    t = x.shape[0]
    tm = OUT_TM
    cond_of = _cond_index(latent, tm)
    row = lambda width: pl.BlockSpec((tm, width), lambda i: (i, 0))
    return pl.pallas_call(
        _out_proj_kernel,
        grid=(t // tm,),
        in_specs=[row(ATT_W), row(ML_W), row(DF_W),
                  pl.BlockSpec((None, D_MODEL, D_MODEL), lambda i: (layer, 0, 0), pipeline_mode=pl.Buffered(1)),
                  row(D_MODEL),
                  pl.BlockSpec((None, 6, D_MODEL), lambda i: (cond_of(i), 0, 0)),
                  pl.BlockSpec((4, D_MODEL), lambda i: (0, 0))],
        out_specs=row(D_MODEL),
        out_shape=jax.ShapeDtypeStruct((t, D_MODEL), F32),
        compiler_params=_params(1),
        name="out_proj",
    )(att, ml, dif, w_out_b, x, mod_l, gains)


def _ffn_kernel(x_ref, mod_ref, g_ref, w1_ref, w2_ref, o_ref, hn_ref):
    f = pl.program_id(1)

    @pl.when(f == 0)
    def _():
        hn = _rms(x_ref[...], g_ref[2:3, :]) * (1.0 + mod_ref[4:5, :]) + mod_ref[3:4, :]
        hn_ref[...] = hn.astype(hn_ref.dtype)
        o_ref[...] = jnp.zeros_like(o_ref)

    h = jnp.maximum(_dot(hn_ref[...], w1_ref[...]), 0.0)
    o_ref[...] += _dot((h * h).astype(BF16), w2_ref[...])

    @pl.when(f == pl.num_programs(1) - 1)
    def _():
        o_ref[...] = x_ref[...] + mod_ref[5:6, :] * _rms(o_ref[...], g_ref[3:4, :])


def _ffn(layer, x, mod_l, gains, w1_b, w2_b, latent):
    t = x.shape[0]
    tm, tf = FFN_TM, FFN_TF
    cond_of = _cond_index(latent, tm)
    return pl.pallas_call(
        _ffn_kernel,
        grid=(t // tm, D_FF // tf),
        in_specs=[pl.BlockSpec((tm, D_MODEL), lambda i, f: (i, 0)),
                  pl.BlockSpec((None, 6, D_MODEL), lambda i, f: (cond_of(i), 0, 0)),
                  pl.BlockSpec((4, D_MODEL), lambda i, f: (0, 0)),
                  pl.BlockSpec((None, D_MODEL, tf), lambda i, f: (layer, 0, f)),
                  pl.BlockSpec((None, tf, D_MODEL), lambda i, f: (layer, f, 0))],
        out_specs=pl.BlockSpec((tm, D_MODEL), lambda i, f: (i, 0)),
        out_shape=jax.ShapeDtypeStruct((t, D_MODEL), F32),
        scratch_shapes=[pltpu.VMEM((tm, D_MODEL), BF16)],
        compiler_params=_params(2),
        name="ffn",
    )(x, mod_l, gains, w1_b, w2_b)


def _rope_tables():
    def angles(dim):
        rows = DEC_SEQ // GRID_W
        row_idx = jnp.repeat(jnp.arange(rows), GRID_W).astype(F32)
        col_idx = jnp.tile(jnp.arange(GRID_W), rows).astype(F32)
        n_freq = dim // 4
        inv = ROPE_THETA ** (-jnp.arange(n_freq, dtype=F32) / n_freq)
        ang = jnp.concatenate([row_idx[:, None] * inv, col_idx[:, None] * inv], axis=-1)
        return jnp.cos(ang), jnp.sin(ang)

    c, s = angles(HEAD_DIM)
    c2, s2 = angles(DIFF_QK_DIM)
    c128 = jnp.concatenate([c, c], axis=-1)
    s128 = jnp.concatenate([-s, s], axis=-1)
    c64 = jnp.concatenate([c2, c2, c2, c2], axis=-1)
    s64 = jnp.concatenate([-s2, s2, -s2, s2], axis=-1)
    return c128, s128, c64, s64


PACK_TK = 256
GATE_COL0 = OFF_MO + ML_W


def _pack_kernel(w_ref, o_ref):
    o_ref[:, :GATE_COL0] = w_ref[:, :GATE_COL0].astype(BF16)
    o_ref[:, GATE_COL0:OFF_MG] = w_ref[:, GATE_COL0 + N_GATES:].astype(BF16)
    pad = jnp.zeros((w_ref.shape[0], LANES - N_GATES), F32)
    o_ref[:, OFF_MG:] = jnp.concatenate([w_ref[:, GATE_COL0:GATE_COL0 + N_GATES], pad], axis=1).astype(BF16)


def _pack_w_in(w_in):
    in_w = w_in.shape[-1]
    return pl.pallas_call(
        _pack_kernel,
        grid=(DEPTH, D_MODEL // PACK_TK),
        in_specs=[pl.BlockSpec((None, PACK_TK, in_w), lambda l, i: (l, i, 0))],
        out_specs=pl.BlockSpec((None, PACK_TK, IN_W_PACKED), lambda l, i: (l, i, 0)),
        out_shape=jax.ShapeDtypeStruct((DEPTH, D_MODEL, IN_W_PACKED), BF16),
        compiler_params=_params(2),
        name="pack_w_in",
    )(w_in)


def kernel(x_prompt, x_sample, c, cache_gqa_k, cache_gqa_v, cache_diff_k, cache_diff_v, state_mlstm_C,
           state_mlstm_n, state_mlstm_m, c_ctx, w_ada, b_ada, norm_gain, w_in, w_out, qk_gain,
           mlstm_gate_bias, mlstm_head_gain, diff_lambda, diff_head_gain, w_ff1, w_ff2):
    cond3 = jnp.concatenate([c_ctx[None, :], c], axis=0)
    mod = _modulation(cond3, w_ada, b_ada)

    w_in_p = _pack_w_in(w_in)
    w_out_b = w_out.astype(BF16)
    w1_b = w_ff1.astype(BF16)
    w2_b = w_ff2.astype(BF16)
    gate_bias = jnp.pad(mlstm_gate_bias.reshape(DEPTH, 1, N_GATES), ((0, 0), (0, 0), (0, LANES - N_GATES)))
    diff_gain = diff_head_gain.reshape(DEPTH, 1, HEAD_DIM)
    rope = _rope_tables()

    state_n6 = state_mlstm_n.reshape(DEC_BATCH, DEPTH, 2, MLSTM_HEADS, 1, HEAD_DIM)
    state_m6 = jnp.broadcast_to(state_mlstm_m[..., None, None], (DEC_BATCH, DEPTH, 2, MLSTM_HEADS, 1, LANES))

    xp = x_prompt.reshape(T_CTX, D_MODEL)
    xs = x_sample.reshape(T_LAT, D_MODEL)
    new = [[] for _ in range(7)]

    for l in range(DEPTH):
        (q, k, v, mq, mk, mv, og, dq, dk, dv, gt, gT) = _in_proj(
            l, xp, mod[l], norm_gain[l], w_in_p, qk_gain[l], gate_bias[l], None, False)
        att, ml, dif, C, n, m = _ctx_mix(l, q, k, v, mq, mk, mv, og, dq, dk, dv, gt, gT,
                                         mlstm_head_gain[l], diff_gain[l], diff_lambda[l])
        xp = _out_proj(l, att, ml, dif, w_out_b, xp, mod[l], norm_gain[l], False)
        xp = _ffn(l, xp, mod[l], norm_gain[l], w1_b, w2_b, False)
        new[0].append(k.reshape(BATCH, SEQ, ATT_KV_HEADS, HEAD_DIM))
        new[1].append(v.reshape(BATCH, SEQ, ATT_KV_HEADS, HEAD_DIM))
        new[2].append(dk.reshape(BATCH, SEQ, DIFF_HEADS, HEAD_DIM))
        new[3].append(dv.reshape(BATCH, SEQ, DIFF_HEADS, HEAD_DIM))
        new[4].append(C)
        new[5].append(n.reshape(BATCH, 2, MLSTM_HEADS, HEAD_DIM))
        new[6].append(m[:, :, 0].reshape(BATCH, 2, MLSTM_HEADS))

        (q, k, v, mq, mk, mv, og, dq, dk, dv, gt, gT) = _in_proj(
            l, xs, mod[l], norm_gain[l], w_in_p, qk_gain[l], gate_bias[l], rope, True)
        as_keys = lambda new_rows, cache, width: jnp.concatenate(
            [new_rows.reshape(DEC_BATCH, DEC_SEQ, width),
             cache[:, l].reshape(DEC_BATCH, PAST_LEN, width).astype(BF16)], axis=1)
        att = _lat_gqa(q, as_keys(k, cache_gqa_k, KV_W), as_keys(v, cache_gqa_v, KV_W))
        dif = _lat_diff(l, dq, as_keys(dk, cache_diff_k, DF_W), as_keys(dv, cache_diff_v, DF_W),
                        diff_gain[l], diff_lambda[l])
        ml = _lat_mlstm(l, mq, mk, mv, og, gt, gT, state_mlstm_C, state_n6, state_m6, mlstm_head_gain[l])
        xs = _out_proj(l, att, ml, dif, w_out_b, xs, mod[l], norm_gain[l], True)
        xs = _ffn(l, xs, mod[l], norm_gain[l], w1_b, w2_b, True)

    outs = [jnp.stack(lst, axis=1) for lst in new]
    return (xp.reshape(BATCH, SEQ, D_MODEL), xs.reshape(DEC_BATCH, DEC_SEQ, D_MODEL), *outs)
```

```python
import functools
import math

import jax
import jax.numpy as jnp
from jax import lax
from jax.experimental import pallas as pl
from jax.experimental.pallas import tpu as pltpu

F32 = jnp.float32
BF16 = jnp.bfloat16

D_MODEL = 2048
BATCH = 32
SEQ = 256
DEPTH = 4
DEC_BATCH = 2
DEC_SEQ = 4096
PAST_LEN = 256
GRID_W = 64
HEAD_DIM = 128
ATT_W = 1024
ML_W = 512
DF_W = 512
ATT_HEADS = 8
ATT_KV_HEADS = 2
KV_W = 256
MLSTM_HEADS = 4
DIFF_HEADS = 4
DIFF_QK_DIM = 64
N_GATES = 16
D_FF = 8192
CHUNK = 128
ROPE_THETA = 10000.0
EPS = 1e-6
LOG2E = 1.4426950408889634
LANES = 128

T_CTX = BATCH * SEQ
T_LAT = DEC_BATCH * DEC_SEQ
KEYS_LAT = DEC_SEQ + PAST_LEN

OFF_AQ, OFF_AK, OFF_AV = 0, 1024, 1280
OFF_MQ, OFF_MK, OFF_MV, OFF_MO = 1536, 2048, 2560, 3072
OFF_DQ, OFF_DK, OFF_DV = 3584, 4096, 4608
OFF_MG = 5120
IN_W_PACKED = OFF_MG + LANES

VMEM_LIMIT = 52 * 1024 * 1024


def _dot(a, b):
    return jnp.dot(a, b, preferred_element_type=F32)


def _dot_nt(a, b):
    return lax.dot_general(a, b, (((1,), (1,)), ((), ())), preferred_element_type=F32)


def _dot_tn(a, b):
    return lax.dot_general(a, b, (((0,), (0,)), ((), ())), preferred_element_type=F32)


def _rms(x, gain):
    ms = jnp.mean(x * x, axis=-1, keepdims=True)
    return x * lax.rsqrt(ms + EPS) * gain


def _sigmoid(x):
    return 1.0 / (1.0 + jnp.exp(-x))


def _log_sigmoid(x):
    return jnp.minimum(x, 0.0) - jnp.log(1.0 + jnp.exp(-jnp.abs(x)))


def _params(n_axes):
    return pltpu.CompilerParams(dimension_semantics=("arbitrary",) * n_axes,
                                vmem_limit_bytes=VMEM_LIMIT)


MOD_TN = 1024
MOD_KC = 64


def _mod_kernel(c_ref, w_ref, b_ref, o_ref, s_ref):
    @pl.when((pl.program_id(0) == 0) & (pl.program_id(1) == 0))
    def _():
        cb = c_ref[...]
        s_ref[...] = cb * _sigmoid(cb)

    tn = w_ref.shape[1]
    reps = tn // LANES

    def body(g, accs):
        rows = pl.ds(pl.multiple_of(g * MOD_KC, MOD_KC), MOD_KC)
        wk = w_ref[rows, :].reshape(MOD_KC // 8, 8, tn)
        out = []
        for r in range(3):
            sk = s_ref[r, rows, :].reshape(MOD_KC // 8, 8, LANES)
            out.append(accs[r] + jnp.sum(wk * jnp.concatenate([sk] * reps, axis=2), axis=0))
        return tuple(out)

    zero = jnp.zeros((8, tn), F32)
    accs = lax.fori_loop(0, D_MODEL // MOD_KC, body, (zero, zero, zero))
    rows = [jnp.sum(a, axis=0, keepdims=True) + b_ref[...] for a in accs]
    rows.append(jnp.zeros((5, tn), F32))
    o_ref[...] = jnp.concatenate(rows, axis=0)


def _modulation(cond3, w_ada, b_ada):
    cb = jnp.broadcast_to(cond3[:, :, None], (3, D_MODEL, LANES))
    n = 6 * D_MODEL
    out = pl.pallas_call(
        _mod_kernel,
        grid=(DEPTH, n // MOD_TN),
        in_specs=[
            pl.BlockSpec((3, D_MODEL, LANES), lambda l, j: (0, 0, 0)),
            pl.BlockSpec((None, D_MODEL, MOD_TN), lambda l, j: (l, 0, j)),
            pl.BlockSpec((None, 1, MOD_TN), lambda l, j: (l, 0, j)),
        ],
        out_specs=pl.BlockSpec((None, 8, MOD_TN), lambda l, j: (l, 0, j)),
        out_shape=jax.ShapeDtypeStruct((DEPTH, 8, n), F32),
        scratch_shapes=[pltpu.VMEM((3, D_MODEL, LANES), F32)],
        compiler_params=_params(2),
        name="adaln_mod",
    )(cb, w_ada, b_ada.reshape(DEPTH, 1, n))
    return out[:, :3].reshape(DEPTH, 3, 6, D_MODEL)


IN_TM = 256


def _rope_full(a, c, s):
    return a * c + pltpu.roll(a, 64, 1) * s


def _rope_two_maps(a, c, s):
    lane = lax.broadcasted_iota(jnp.int32, a.shape, 1)
    partner = jnp.where((lane % 64) < 32, pltpu.roll(a, 96, 1), pltpu.roll(a, 32, 1))
    return a * c + partner * s


def _in_proj_kernel(latent, *refs):
    if latent:
        (x_ref, mod_ref, g_ref, w_ref, qkg_ref, gb_ref, c128_ref, s128_ref, c64_ref, s64_ref,
         q_ref, k_ref, v_ref, mq_ref, mk_ref, mv_ref, og_ref, dq_ref, dk_ref, dv_ref, gt_ref, gT_ref) = refs
    else:
        (x_ref, mod_ref, g_ref, w_ref, qkg_ref, gb_ref,
         q_ref, k_ref, v_ref, mq_ref, mk_ref, mv_ref, og_ref, dq_ref, dk_ref, dv_ref, gt_ref, gT_ref) = refs

    x = x_ref[...]
    hn = _rms(x, g_ref[0:1, :]) * (1.0 + mod_ref[1:2, :]) + mod_ref[0:1, :]
    hnb = hn.astype(BF16)

    def proj(off, width):
        return _dot(hnb, w_ref[:, off:off + width])

    q_scale = HEAD_DIM ** -0.5 * LOG2E
    aq = proj(OFF_AQ, ATT_W)
    for h in range(ATT_HEADS):
        a = _rms(aq[:, h * HEAD_DIM:(h + 1) * HEAD_DIM], qkg_ref[0:1, :])
        if latent:
            a = _rope_full(a, c128_ref[...], s128_ref[...])
        q_ref[:, h * HEAD_DIM:(h + 1) * HEAD_DIM] = (a * q_scale).astype(q_ref.dtype)

    ak = proj(OFF_AK, KV_W)
    for h in range(ATT_KV_HEADS):
        a = _rms(ak[:, h * HEAD_DIM:(h + 1) * HEAD_DIM], qkg_ref[1:2, :])
        if latent:
            a = _rope_full(a, c128_ref[...], s128_ref[...])
        k_ref[:, h * HEAD_DIM:(h + 1) * HEAD_DIM] = a.astype(k_ref.dtype)

    v_ref[...] = proj(OFF_AV, KV_W).astype(v_ref.dtype)
    mq_ref[...] = proj(OFF_MQ, ML_W).astype(mq_ref.dtype)
    mk_ref[...] = (proj(OFF_MK, ML_W) * (HEAD_DIM ** -0.5)).astype(mk_ref.dtype)
    mv_ref[...] = proj(OFF_MV, ML_W).astype(mv_ref.dtype)
    og_ref[...] = _sigmoid(proj(OFF_MO, ML_W)).astype(og_ref.dtype)

    dq = proj(OFF_DQ, DF_W)
    dk = proj(OFF_DK, DF_W)
    dq_scale = DIFF_QK_DIM ** -0.5 * LOG2E
    for h in range(DIFF_HEADS):
        sl = slice(h * HEAD_DIM, (h + 1) * HEAD_DIM)
        a, b = dq[:, sl], dk[:, sl]
        if latent:
            a = _rope_two_maps(a, c64_ref[...], s64_ref[...])
            b = _rope_two_maps(b, c64_ref[...], s64_ref[...])
        dq_ref[:, sl] = (a * dq_scale).astype(dq_ref.dtype)
        dk_ref[:, sl] = b.astype(dk_ref.dtype)
    dv_ref[...] = proj(OFF_DV, DF_W).astype(dv_ref.dtype)

    g = proj(OFF_MG, LANES) + gb_ref[...]
    lane = lax.broadcasted_iota(jnp.int32, g.shape, 1)
    gates = jnp.where((lane & 4) != 0, _log_sigmoid(g), g)
    gt_ref[...] = gates
    for j in range(gates.shape[0] // CHUNK):
        gT_ref[j] = gates[j * CHUNK:(j + 1) * CHUNK, :].T


def _in_proj(layer, x, mod_l, gains, w_in_p, qk_gain_l, gate_bias_row, rope, latent):
    t = x.shape[0]
    tm = IN_TM
    n_tiles = t // tm
    per_seq = DEC_SEQ // tm
    if latent:
        cond_of = lambda i: 1 + i // per_seq
    else:
        cond_of = lambda i: 0
    row = lambda width: pl.BlockSpec((tm, width), lambda i: (i, 0))
    in_specs = [
        row(D_MODEL),
        pl.BlockSpec((None, 6, D_MODEL), lambda i: (cond_of(i), 0, 0)),
        pl.BlockSpec((4, D_MODEL), lambda i: (0, 0)),
        pl.BlockSpec((None, D_MODEL, IN_W_PACKED), lambda i: (layer, 0, 0), pipeline_mode=pl.Buffered(1)),
        pl.BlockSpec((2, HEAD_DIM), lambda i: (0, 0)),
        pl.BlockSpec((1, LANES), lambda i: (0, 0)),
    ]
    args = [x, mod_l, gains, w_in_p, qk_gain_l, gate_bias_row]
    if latent:
        in_specs += [pl.BlockSpec((tm, LANES), lambda i: (i % per_seq, 0))] * 4
        args += list(rope)
    kv_dt = BF16 if latent else F32
    widths_dtypes = [(ATT_W, BF16), (KV_W, kv_dt), (KV_W, kv_dt),
                     (ML_W, BF16), (ML_W, BF16), (ML_W, BF16), (ML_W, BF16),
                     (DF_W, BF16), (DF_W, kv_dt), (DF_W, kv_dt), (LANES, F32)]
    return pl.pallas_call(
        functools.partial(_in_proj_kernel, latent),
        grid=(n_tiles,),
        in_specs=in_specs,
        out_specs=[row(w) for w, _ in widths_dtypes]
        + [pl.BlockSpec((tm // CHUNK, LANES, CHUNK), lambda i: (i, 0, 0))],
        out_shape=[jax.ShapeDtypeStruct((t, w), dt) for w, dt in widths_dtypes]
        + [jax.ShapeDtypeStruct((t // CHUNK, LANES, CHUNK), F32)],
        compiler_params=_params(1),
        name="in_proj_lat" if latent else "in_proj_ctx",
    )(*args)


ATT_SUB = 128


def _with_ones(v):
    return jnp.concatenate([v, jnp.ones_like(v)], axis=1)


def _attend_many(load_q, n, k, va, emit):
    scores, probs = {}, {}
    for t in range(n + 2):
        if t < n:
            scores[t] = _dot_nt(load_q(t), k)
        if 0 <= t - 1 < n:
            s = scores.pop(t - 1)
            probs[t - 1] = jnp.exp2(s - jnp.max(s, axis=-1, keepdims=True)).astype(BF16)
        if 0 <= t - 2 < n:
            oa = _dot(probs.pop(t - 2), va)
            emit(t - 2, oa[:, :HEAD_DIM] / oa[:, HEAD_DIM:])


def _diff_lambda(lam_ref, lam_init):
    lv = lam_ref[...]
    a = jnp.sum(lv[0:1, :] * lv[1:2, :], axis=-1, keepdims=True)
    b = jnp.sum(lv[2:3, :] * lv[3:4, :], axis=-1, keepdims=True)
    return jnp.exp(a) - jnp.exp(b) + lam_init


def _diff_attend_many(load_q, n, k, va, lam, gain, lam_init, emit):
    def load_map(c):
        q = load_q(c // 2)
        lane = lax.broadcasted_iota(jnp.int32, q.shape, 1)
        keep = (lane < DIFF_QK_DIM) if c % 2 == 0 else (lane >= DIFF_QK_DIM)
        return jnp.where(keep, q, jnp.zeros_like(q))

    first = {}

    def combine(c, o):
        if c % 2 == 0:
            first[c // 2] = o
        else:
            emit(c // 2, _rms(first.pop(c // 2) - lam * o, gain) * (1.0 - lam_init))

    _attend_many(load_map, 2 * n, k, va, combine)


def _tri(lower):
    r = lax.broadcasted_iota(jnp.int32, (CHUNK, CHUNK), 0)
    c = lax.broadcasted_iota(jnp.int32, (CHUNK, CHUNK), 1)
    return (c <= r) if lower else (c >= r)


def _scan_masks():
    masks = (_tri(True), _tri(False))
    return masks, tuple(jnp.where(mk, 1.0, 0.0).astype(BF16) for mk in masks)


def _split3(g):
    g1 = g.astype(BF16)
    r1 = g - g1.astype(F32)
    g2 = r1.astype(BF16)
    g3 = (r1 - g2.astype(F32)).astype(BF16)
    return g1, g2, g3


def _gate_cumsums(g_parts, gT_parts, tris, d):
    b = sum(_dot(tris[d], p) for p in g_parts)
    bT = sum(_dot(p, tris[1 - d]) for p in gT_parts)
    return b, bT


def _mlstm_chunks(items, masks):
    gate = []
    for (q, k, va, bcol, brow, icol, irow, d, CN, m) in items:
        bfull = jnp.broadcast_to(bcol, (CHUNK, CHUNK))
        ifull = jnp.broadcast_to(icol, (CHUNK, CHUNK))
        logw = jnp.where(masks[d], bfull - brow + irow, -jnp.inf)
        inter = bfull + m
        m_row = jnp.maximum(inter, jnp.max(logw, axis=1, keepdims=True))
        gate.append((bfull, ifull, jnp.exp(logw - m_row), jnp.exp(inter - m_row), m_row))
    first = [(_dot_nt(q, k), _dot(q, CN.astype(BF16))) for (q, k, va, *_, CN, m) in items]
    second = [_dot((qk * p).astype(BF16), it[2]) for it, (_, _, p, _, _), (qk, _) in zip(items, gate, first)]
    out = []
    for (q, k, va, bcol, brow, icol, irow, d, CN, m), (bfull, ifull, _, a_int, m_row), (_, qc), nd in zip(
            items, gate, first, second):
        num = nd[:, :HEAD_DIM] + a_int * qc[:, :HEAD_DIM]
        den = nd[:, HEAD_DIM:] + a_int * qc[:, HEAD_DIM:]
        h = num / jnp.maximum(jnp.abs(den), jnp.exp(-m_row))
        last = CHUNK - 1 if d == 0 else 0
        m_new = m_row[last:last + 1, :]
        btot = bfull[last:last + 1, :]
        w_end = jnp.exp(btot - bfull + ifull - m_new)
        decay = jnp.exp(btot + m - m_new)
        kw = (k.astype(F32) * w_end).astype(BF16)
        out.append((h, jnp.concatenate([decay, decay], axis=1) * CN + _dot_tn(kw, va), m_new))
    return out


def _gate_cols(d, h):
    return d * 8 + h, d * 8 + 4 + h


def _state_in(C, n_row):
    n_col = jnp.broadcast_to(n_row, (HEAD_DIM, HEAD_DIM)).T
    return jnp.concatenate([C, n_col], axis=1)


def _ctx_mix_kernel(lam_init, q_ref, k_ref, v_ref, mq_ref, mk_ref, mv_ref, og_ref, dq_ref, dk_ref,
                    dv_ref, gt_ref, gT_ref, mlg_ref, dfg_ref, lam_ref,
                    att_ref, ml_ref, dif_ref, C_ref, n_ref, m_ref):
    n_tok = SEQ
    rep = ATT_HEADS // ATT_KV_HEADS
    for g in range(ATT_KV_HEADS):
        sl = slice(g * HEAD_DIM, (g + 1) * HEAD_DIM)
        kb = k_ref[:, sl].astype(BF16)
        va = _with_ones(v_ref[:, sl].astype(BF16))
        head = lambda r: slice((g * rep + r) * HEAD_DIM, (g * rep + r + 1) * HEAD_DIM)

        def put_att(r, o):
            att_ref[:, head(r)] = o.astype(att_ref.dtype)

        _attend_many(lambda r: q_ref[:, head(r)], rep, kb, va, put_att)

    lam = _diff_lambda(lam_ref, lam_init)
    for h in range(DIFF_HEADS):
        sl = slice(h * HEAD_DIM, (h + 1) * HEAD_DIM)

        def put_dif(_, d):
            dif_ref[:, sl] = d.astype(dif_ref.dtype)

        _diff_attend_many(lambda _: dq_ref[:, sl], 1, dk_ref[:, sl].astype(BF16),
                          _with_ones(dv_ref[:, sl].astype(BF16)), lam, dfg_ref[...], lam_init, put_dif)

    n_chunks = n_tok // CHUNK
    masks, tris = _scan_masks()
    forms = []
    for c in range(n_chunks):
        g = gt_ref[c * CHUNK:(c + 1) * CHUNK, :]
        gT = gT_ref[c]
        gp, gTp = _split3(g), _split3(gT)
        forms.append((g, gT) + tuple(_gate_cumsums(gp, gTp, tris, d) for d in range(2)))
    scans = [(d, h) for d in range(2) for h in range(MLSTM_HEADS)]
    state = [(jnp.zeros((HEAD_DIM, 2 * HEAD_DIM), F32), jnp.full((1, LANES), -jnp.inf, F32)) for _ in scans]
    hsum = {}
    for t in range(n_chunks):
        items = []
        for (d, h), (CN, m) in zip(scans, state):
            c = t if d == 0 else n_chunks - 1 - t
            g, gT = forms[c][0], forms[c][1]
            b, bT = forms[c][2 + d]
            ci, cf = _gate_cols(d, h)
            rows = slice(c * CHUNK, (c + 1) * CHUNK)
            sl = slice(h * HEAD_DIM, (h + 1) * HEAD_DIM)
            items.append((mq_ref[rows, sl], mk_ref[rows, sl], _with_ones(mv_ref[rows, sl]),
                          b[:, cf:cf + 1], bT[cf:cf + 1, :], g[:, ci:ci + 1], gT[ci:ci + 1, :], d, CN, m))
        res = _mlstm_chunks(items, masks)
        state = [(CN, m) for _, CN, m in res]
        for (d, h), (hc, _, _) in zip(scans, res):
            c = t if d == 0 else n_chunks - 1 - t
            hsum[(h, c)] = hc if (h, c) not in hsum else hsum[(h, c)] + hc
    for (d, h), (CN, m) in zip(scans, state):
        o = d * MLSTM_HEADS + h
        C_ref[d, h] = CN[:, :HEAD_DIM]
        n_ref[o:o + 1, :] = CN[:, HEAD_DIM:].T[0:1, :]
        m_ref[o:o + 1, :] = m
    for h in range(MLSTM_HEADS):
        sl = slice(h * HEAD_DIM, (h + 1) * HEAD_DIM)
        for c in range(n_chunks):
            rows = slice(c * CHUNK, (c + 1) * CHUNK)
            y = _rms(hsum[(h, c)], mlg_ref[h:h + 1, :]) * og_ref[rows, sl].astype(F32)
            ml_ref[rows, sl] = y.astype(ml_ref.dtype)


def _ctx_mix(layer, q, k, v, mq, mk, mv, og, dq, dk, dv, gt, gT, ml_gain, diff_gain, diff_lambda):
    lam_init = 0.8 - 0.6 * math.exp(-0.3 * layer)
    row = lambda width: pl.BlockSpec((SEQ, width), lambda b: (b, 0))
    full = lambda a: pl.BlockSpec(a.shape, lambda b: (0,) * a.ndim)
    nh = 2 * MLSTM_HEADS
    return pl.pallas_call(
        functools.partial(_ctx_mix_kernel, lam_init),
        grid=(BATCH,),
        in_specs=[row(ATT_W), row(KV_W), row(KV_W), row(ML_W), row(ML_W), row(ML_W), row(ML_W),
                  row(DF_W), row(DF_W), row(DF_W), row(LANES),
                  pl.BlockSpec((SEQ // CHUNK, LANES, CHUNK), lambda b: (b, 0, 0)),
                  full(ml_gain), full(diff_gain), full(diff_lambda)],
        out_specs=[row(ATT_W), row(ML_W), row(DF_W),
                   pl.BlockSpec((None, 2, MLSTM_HEADS, HEAD_DIM, HEAD_DIM), lambda b: (b, 0, 0, 0, 0)),
                   pl.BlockSpec((None, nh, HEAD_DIM), lambda b: (b, 0, 0)),
                   pl.BlockSpec((None, nh, LANES), lambda b: (b, 0, 0))],
        out_shape=[jax.ShapeDtypeStruct((T_CTX, ATT_W), BF16),
                   jax.ShapeDtypeStruct((T_CTX, ML_W), BF16),
                   jax.ShapeDtypeStruct((T_CTX, DF_W), BF16),
                   jax.ShapeDtypeStruct((BATCH, 2, MLSTM_HEADS, HEAD_DIM, HEAD_DIM), F32),
                   jax.ShapeDtypeStruct((BATCH, nh, HEAD_DIM), F32),
                   jax.ShapeDtypeStruct((BATCH, nh, LANES), F32)],
        compiler_params=_params(1),
        name="ctx_mix",
    )(q, k, v, mq, mk, mv, og, dq, dk, dv, gt, gT, ml_gain, diff_gain, diff_lambda)


GQA_TQ = 512
DIFF_TQ = 1024


def _lat_gqa_kernel(q_ref, k_ref, v_ref, o_ref):
    blocks = [(slice(j * ATT_SUB, (j + 1) * ATT_SUB), slice(r * HEAD_DIM, (r + 1) * HEAD_DIM))
              for r in range(ATT_HEADS // ATT_KV_HEADS) for j in range(q_ref.shape[0] // ATT_SUB)]

    def put(i, o):
        o_ref[blocks[i]] = o.astype(o_ref.dtype)

    _attend_many(lambda i: q_ref[blocks[i]], len(blocks), k_ref[...], _with_ones(v_ref[...]), put)


def _lat_gqa(q, k_full, v_full):
    rep = ATT_HEADS // ATT_KV_HEADS
    nq = DEC_SEQ // GQA_TQ
    qspec = pl.BlockSpec((GQA_TQ, rep * HEAD_DIM), lambda b, g, i: (b * nq + i, g))
    kvspec = pl.BlockSpec((None, KEYS_LAT, HEAD_DIM), lambda b, g, i: (b, 0, g))
    return pl.pallas_call(
        _lat_gqa_kernel,
        grid=(DEC_BATCH, ATT_KV_HEADS, nq),
        in_specs=[qspec, kvspec, kvspec],
        out_specs=qspec,
        out_shape=jax.ShapeDtypeStruct((T_LAT, ATT_W), BF16),
        compiler_params=_params(3),
        name="lat_gqa",
    )(q, k_full, v_full)


def _lat_diff_kernel(lam_init, q_ref, k_ref, v_ref, dfg_ref, lam_ref, o_ref):
    lam = _diff_lambda(lam_ref, lam_init)
    rows = lambda j: slice(j * ATT_SUB, (j + 1) * ATT_SUB)

    def put(j, d):
        o_ref[rows(j), :] = d.astype(o_ref.dtype)

    _diff_attend_many(lambda j: q_ref[rows(j), :], q_ref.shape[0] // ATT_SUB, k_ref[...], _with_ones(v_ref[...]),
                      lam, dfg_ref[...], lam_init, put)


def _lat_diff(layer, dq, dk_full, dv_full, diff_gain, diff_lambda):
    lam_init = 0.8 - 0.6 * math.exp(-0.3 * layer)
    nq = DEC_SEQ // DIFF_TQ
    qspec = pl.BlockSpec((DIFF_TQ, HEAD_DIM), lambda b, h, i: (b * nq + i, h))
    kvspec = pl.BlockSpec((None, KEYS_LAT, HEAD_DIM), lambda b, h, i: (b, 0, h))
    full = lambda a: pl.BlockSpec(a.shape, lambda b, h, i: (0,) * a.ndim)
    return pl.pallas_call(
        functools.partial(_lat_diff_kernel, lam_init),
        grid=(DEC_BATCH, DIFF_HEADS, nq),
        in_specs=[qspec, kvspec, kvspec, full(diff_gain), full(diff_lambda)],
        out_specs=qspec,
        out_shape=jax.ShapeDtypeStruct((T_LAT, DF_W), BF16),
        compiler_params=_params(3),
        name="lat_diff",
    )(dq, dk_full, dv_full, diff_gain, diff_lambda)


def _lat_mlstm_kernel(q_ref, k_ref, v_ref, og_ref, gt_ref, gT_ref, C0_ref, n0_ref, m0_ref, mlg_ref,
                      o_ref, hs_ref):
    n_chunks = DEC_SEQ // CHUNK
    masks, tris = _scan_masks()

    def step(second_half, t, carry):
        items, where = [], []
        for d in range(2):
            c = t if d == 0 else n_chunks - 1 - t
            rows = pl.ds(pl.multiple_of(c * CHUNK, CHUNK), CHUNK)
            g = gt_ref[rows, :]
            gT = gT_ref[c]
            b, bT = _gate_cumsums(_split3(g), _split3(gT), tris, d)
            for h in range(MLSTM_HEADS):
                sl = slice(h * HEAD_DIM, (h + 1) * HEAD_DIM)
                ci, cf = _gate_cols(d, h)
                CN, m = carry[d * MLSTM_HEADS + h]
                items.append((q_ref[rows, sl], k_ref[rows, sl], _with_ones(v_ref[rows, sl]),
                              b[:, cf:cf + 1], bT[cf:cf + 1, :], g[:, ci:ci + 1], gT[ci:ci + 1, :], d, CN, m))
                where.append((rows, sl, h))
        res = _mlstm_chunks(items, masks)
        for (rows, sl, h), (hc, _, _) in zip(where, res):
            if second_half:
                y = _rms(hs_ref[rows, sl] + hc, mlg_ref[h:h + 1, :]) * og_ref[rows, sl].astype(F32)
                o_ref[rows, sl] = y.astype(o_ref.dtype)
            else:
                hs_ref[rows, sl] = hc
        return tuple((CN, m) for _, CN, m in res)

    init = tuple((_state_in(C0_ref[d, h], n0_ref[d, h]), m0_ref[d, h])
                 for d in range(2) for h in range(MLSTM_HEADS))
    half = n_chunks // 2
    carry = lax.fori_loop(0, half, functools.partial(step, False), init)
    lax.fori_loop(half, n_chunks, functools.partial(step, True), carry)


def _lat_mlstm(layer, mq, mk, mv, og, gt, gT, state_C, state_n6, state_m6, ml_gain):
    once = pl.Buffered(1)
    hspec = pl.BlockSpec((DEC_SEQ, ML_W), lambda b: (b, 0), pipeline_mode=once)
    st = lambda last2: pl.BlockSpec((None, None, 2, MLSTM_HEADS) + last2, lambda b: (b, layer, 0, 0, 0, 0))
    return pl.pallas_call(
        _lat_mlstm_kernel,
        grid=(DEC_BATCH,),
        in_specs=[hspec, hspec, hspec, hspec,
                  pl.BlockSpec((DEC_SEQ, LANES), lambda b: (b, 0), pipeline_mode=once),
                  pl.BlockSpec((DEC_SEQ // CHUNK, LANES, CHUNK), lambda b: (b, 0, 0), pipeline_mode=once),
                  st((HEAD_DIM, HEAD_DIM)), st((1, HEAD_DIM)), st((1, LANES)),
                  pl.BlockSpec(ml_gain.shape, lambda b: (0, 0))],
        out_specs=pl.BlockSpec((DEC_SEQ, ML_W), lambda b: (b, 0)),
        out_shape=jax.ShapeDtypeStruct((T_LAT, ML_W), BF16),
        scratch_shapes=[pltpu.VMEM((DEC_SEQ, ML_W), F32)],
        compiler_params=_params(1),
        name="lat_mlstm",
    )(mq, mk, mv, og, gt, gT, state_C, state_n6, state_m6, ml_gain)


OUT_TM = 512
FFN_TM = 512
FFN_TF = 1024


def _cond_index(latent, tm):
    per_seq = DEC_SEQ // tm
    if latent:
        return lambda i: 1 + i // per_seq
    return lambda i: 0


def _out_proj_kernel(att_ref, ml_ref, dif_ref, w_ref, x_ref, mod_ref, g_ref, o_ref):
    mix = (_dot(att_ref[...], w_ref[0:ATT_W, :])
           + _dot(ml_ref[...], w_ref[ATT_W:ATT_W + ML_W, :])
           + _dot(dif_ref[...], w_ref[ATT_W + ML_W:, :]))
    o_ref[...] = x_ref[...] + mod_ref[2:3, :] * _rms(mix, g_ref[1:2, :])


def _out_proj(layer, att, ml, dif, w_out_b, x, mod_l, gains, latent):
    t = x.shape[0]
    tm = OUT_TM
    cond_of = _cond_index(latent, tm)
    row = lambda width: pl.BlockSpec((tm, width), lambda i: (i, 0))
    return pl.pallas_call(
        _out_proj_kernel,
        grid=(t // tm,),
        in_specs=[row(ATT_W), row(ML_W), row(DF_W),
                  pl.BlockSpec((None, D_MODEL, D_MODEL), lambda i: (layer, 0, 0), pipeline_mode=pl.Buffered(1)),
                  row(D_MODEL),
                  pl.BlockSpec((None, 6, D_MODEL), lambda i: (cond_of(i), 0, 0)),
                  pl.BlockSpec((4, D_MODEL), lambda i: (0, 0))],
        out_specs=row(D_MODEL),
        out_shape=jax.ShapeDtypeStruct((t, D_MODEL), F32),
        compiler_params=_params(1),
        name="out_proj",
    )(att, ml, dif, w_out_b, x, mod_l, gains)


def _ffn_kernel(x_ref, mod_ref, g_ref, w1_ref, w2_ref, o_ref, hn_ref):
    f = pl.program_id(1)

    @pl.when(f == 0)
    def _():
        hn = _rms(x_ref[...], g_ref[2:3, :]) * (1.0 + mod_ref[4:5, :]) + mod_ref[3:4, :]
        hn_ref[...] = hn.astype(hn_ref.dtype)
        o_ref[...] = jnp.zeros_like(o_ref)

    h = jnp.maximum(_dot(hn_ref[...], w1_ref[...]), 0.0)
    o_ref[...] += _dot((h * h).astype(BF16), w2_ref[...])

    @pl.when(f == pl.num_programs(1) - 1)
    def _():
        o_ref[...] = x_ref[...] + mod_ref[5:6, :] * _rms(o_ref[...], g_ref[3:4, :])


def _ffn(layer, x, mod_l, gains, w1_b, w2_b, latent):
    t = x.shape[0]
    tm, tf = FFN_TM, FFN_TF
    cond_of = _cond_index(latent, tm)
    return pl.pallas_call(
        _ffn_kernel,
        grid=(t // tm, D_FF // tf),
        in_specs=[pl.BlockSpec((tm, D_MODEL), lambda i, f: (i, 0)),
                  pl.BlockSpec((None, 6, D_MODEL), lambda i, f: (cond_of(i), 0, 0)),
                  pl.BlockSpec((4, D_MODEL), lambda i, f: (0, 0)),
                  pl.BlockSpec((None, D_MODEL, tf), lambda i, f: (layer, 0, f)),
                  pl.BlockSpec((None, tf, D_MODEL), lambda i, f: (layer, f, 0))],
        out_specs=pl.BlockSpec((tm, D_MODEL), lambda i, f: (i, 0)),
        out_shape=jax.ShapeDtypeStruct((t, D_MODEL), F32),
        scratch_shapes=[pltpu.VMEM((tm, D_MODEL), BF16)],
        compiler_params=_params(2),
        name="ffn",
    )(x, mod_l, gains, w1_b, w2_b)


def _rope_tables():
    def angles(dim):
        rows = DEC_SEQ // GRID_W
        row_idx = jnp.repeat(jnp.arange(rows), GRID_W).astype(F32)
        col_idx = jnp.tile(jnp.arange(GRID_W), rows).astype(F32)
        n_freq = dim // 4
        inv = ROPE_THETA ** (-jnp.arange(n_freq, dtype=F32) / n_freq)
        ang = jnp.concatenate([row_idx[:, None] * inv, col_idx[:, None] * inv], axis=-1)
        return jnp.cos(ang), jnp.sin(ang)

    c, s = angles(HEAD_DIM)
    c2, s2 = angles(DIFF_QK_DIM)
    c128 = jnp.concatenate([c, c], axis=-1)
    s128 = jnp.concatenate([-s, s], axis=-1)
    c64 = jnp.concatenate([c2, c2, c2, c2], axis=-1)
    s64 = jnp.concatenate([-s2, s2, -s2, s2], axis=-1)
    return c128, s128, c64, s64


PACK_TK = 256
GATE_COL0 = OFF_MO + ML_W


def _pack_kernel(w_ref, o_ref):
    o_ref[:, :GATE_COL0] = w_ref[:, :GATE_COL0].astype(BF16)
    o_ref[:, GATE_COL0:OFF_MG] = w_ref[:, GATE_COL0 + N_GATES:].astype(BF16)
    pad = jnp.zeros((w_ref.shape[0], LANES - N_GATES), F32)
    o_ref[:, OFF_MG:] = jnp.concatenate([w_ref[:, GATE_COL0:GATE_COL0 + N_GATES], pad], axis=1).astype(BF16)


def _pack_w_in(w_in):
    in_w = w_in.shape[-1]
    return pl.pallas_call(
        _pack_kernel,
        grid=(DEPTH, D_MODEL // PACK_TK),
        in_specs=[pl.BlockSpec((None, PACK_TK, in_w), lambda l, i: (l, i, 0))],
        out_specs=pl.BlockSpec((None, PACK_TK, IN_W_PACKED), lambda l, i: (l, i, 0)),
        out_shape=jax.ShapeDtypeStruct((DEPTH, D_MODEL, IN_W_PACKED), BF16),
        compiler_params=_params(2),
        name="pack_w_in",
    )(w_in)


def kernel(x_prompt, x_sample, c, cache_gqa_k, cache_gqa_v, cache_diff_k, cache_diff_v, state_mlstm_C,
           state_mlstm_n, state_mlstm_m, c_ctx, w_ada, b_ada, norm_gain, w_in, w_out, qk_gain,
           mlstm_gate_bias, mlstm_head_gain, diff_lambda, diff_head_gain, w_ff1, w_ff2):
    cond3 = jnp.concatenate([c_ctx[None, :], c], axis=0)
    mod = _modulation(cond3, w_ada, b_ada)

    w_in_p = _pack_w_in(w_in)
    w_out_b = w_out.astype(BF16)
    w1_b = w_ff1.astype(BF16)
    w2_b = w_ff2.astype(BF16)
    gate_bias = jnp.pad(mlstm_gate_bias.reshape(DEPTH, 1, N_GATES), ((0, 0), (0, 0), (0, LANES - N_GATES)))
    diff_gain = diff_head_gain.reshape(DEPTH, 1, HEAD_DIM)
    rope = _rope_tables()

    state_n6 = state_mlstm_n.reshape(DEC_BATCH, DEPTH, 2, MLSTM_HEADS, 1, HEAD_DIM)
    state_m6 = jnp.broadcast_to(state_mlstm_m[..., None, None], (DEC_BATCH, DEPTH, 2, MLSTM_HEADS, 1, LANES))

    xp = x_prompt.reshape(T_CTX, D_MODEL)
    xs = x_sample.reshape(T_LAT, D_MODEL)
    new = [[] for _ in range(7)]

    for l in range(DEPTH):
        (q, k, v, mq, mk, mv, og, dq, dk, dv, gt, gT) = _in_proj(
            l, xp, mod[l], norm_gain[l], w_in_p, qk_gain[l], gate_bias[l], None, False)
        att, ml, dif, C, n, m = _ctx_mix(l, q, k, v, mq, mk, mv, og, dq, dk, dv, gt, gT,
                                         mlstm_head_gain[l], diff_gain[l], diff_lambda[l])
        xp = _out_proj(l, att, ml, dif, w_out_b, xp, mod[l], norm_gain[l], False)
        xp = _ffn(l, xp, mod[l], norm_gain[l], w1_b, w2_b, False)
        new[0].append(k.reshape(BATCH, SEQ, ATT_KV_HEADS, HEAD_DIM))
        new[1].append(v.reshape(BATCH, SEQ, ATT_KV_HEADS, HEAD_DIM))
        new[2].append(dk.reshape(BATCH, SEQ, DIFF_HEADS, HEAD_DIM))
        new[3].append(dv.reshape(BATCH, SEQ, DIFF_HEADS, HEAD_DIM))
        new[4].append(C)
        new[5].append(n.reshape(BATCH, 2, MLSTM_HEADS, HEAD_DIM))
        new[6].append(m[:, :, 0].reshape(BATCH, 2, MLSTM_HEADS))

        (q, k, v, mq, mk, mv, og, dq, dk, dv, gt, gT) = _in_proj(
            l, xs, mod[l], norm_gain[l], w_in_p, qk_gain[l], gate_bias[l], rope, True)
        as_keys = lambda new_rows, cache, width: jnp.concatenate(
            [new_rows.reshape(DEC_BATCH, DEC_SEQ, width),
             cache[:, l].reshape(DEC_BATCH, PAST_LEN, width).astype(BF16)], axis=1)
        att = _lat_gqa(q, as_keys(k, cache_gqa_k, KV_W), as_keys(v, cache_gqa_v, KV_W))
        dif = _lat_diff(l, dq, as_keys(dk, cache_diff_k, DF_W), as_keys(dv, cache_diff_v, DF_W),
                        diff_gain[l], diff_lambda[l])
        ml = _lat_mlstm(l, mq, mk, mv, og, gt, gT, state_mlstm_C, state_n6, state_m6, mlstm_head_gain[l])
        xs = _out_proj(l, att, ml, dif, w_out_b, xs, mod[l], norm_gain[l], True)
        xs = _ffn(l, xs, mod[l], norm_gain[l], w1_b, w2_b, True)

    outs = [jnp.stack(lst, axis=1) for lst in new]
    return (xp.reshape(BATCH, SEQ, D_MODEL), xs.reshape(DEC_BATCH, DEC_SEQ, D_MODEL), *outs)
```

```python
import functools
import math

import jax
import jax.numpy as jnp
from jax import lax
from jax.experimental import pallas as pl
from jax.experimental.pallas import tpu as pltpu

F32 = jnp.float32
BF16 = jnp.bfloat16

D_MODEL = 2048
BATCH = 32
SEQ = 256
DEPTH = 4
DEC_BATCH = 2
DEC_SEQ = 4096
PAST_LEN = 256
GRID_W = 64
HEAD_DIM = 128
ATT_W = 1024
ML_W = 512
DF_W = 512
ATT_HEADS = 8
ATT_KV_HEADS = 2
KV_W = 256
MLSTM_HEADS = 4
DIFF_HEADS = 4
DIFF_QK_DIM = 64
N_GATES = 16
D_FF = 8192
CHUNK = 128
ROPE_THETA = 10000.0
EPS = 1e-6
LOG2E = 1.4426950408889634
LANES = 128

T_CTX = BATCH * SEQ
T_LAT = DEC_BATCH * DEC_SEQ
KEYS_LAT = DEC_SEQ + PAST_LEN

OFF_AQ, OFF_AK, OFF_AV = 0, 1024, 1280
OFF_MQ, OFF_MK, OFF_MV, OFF_MO = 1536, 2048, 2560, 3072
OFF_DQ, OFF_DK, OFF_DV = 3584, 4096, 4608
OFF_MG = 5120
IN_W_PACKED = OFF_MG + LANES

VMEM_LIMIT = 52 * 1024 * 1024


def _dot(a, b):
    return jnp.dot(a, b, preferred_element_type=F32)


def _dot_nt(a, b):
    return lax.dot_general(a, b, (((1,), (1,)), ((), ())), preferred_element_type=F32)


def _dot_tn(a, b):
    return lax.dot_general(a, b, (((0,), (0,)), ((), ())), preferred_element_type=F32)


def _rms(x, gain):
    ms = jnp.mean(x * x, axis=-1, keepdims=True)
    return x * lax.rsqrt(ms + EPS) * gain


def _sigmoid(x):
    return 1.0 / (1.0 + jnp.exp(-x))


def _log_sigmoid(x):
    return jnp.minimum(x, 0.0) - jnp.log(1.0 + jnp.exp(-jnp.abs(x)))


def _params(n_axes):
    return pltpu.CompilerParams(dimension_semantics=("arbitrary",) * n_axes,
                                vmem_limit_bytes=VMEM_LIMIT)


MOD_TN = 1024
MOD_KC = 64


def _mod_kernel(c_ref, w_ref, b_ref, o_ref, s_ref):
    @pl.when((pl.program_id(0) == 0) & (pl.program_id(1) == 0))
    def _():
        cb = c_ref[...]
        s_ref[...] = cb * _sigmoid(cb)

    tn = w_ref.shape[1]
    reps = tn // LANES

    def body(g, accs):
        rows = pl.ds(pl.multiple_of(g * MOD_KC, MOD_KC), MOD_KC)
        wk = w_ref[rows, :].reshape(MOD_KC // 8, 8, tn)
        out = []
        for r in range(3):
            sk = s_ref[r, rows, :].reshape(MOD_KC // 8, 8, LANES)
            out.append(accs[r] + jnp.sum(wk * jnp.concatenate([sk] * reps, axis=2), axis=0))
        return tuple(out)

    zero = jnp.zeros((8, tn), F32)
    accs = lax.fori_loop(0, D_MODEL // MOD_KC, body, (zero, zero, zero))
    rows = [jnp.sum(a, axis=0, keepdims=True) + b_ref[...] for a in accs]
    rows.append(jnp.zeros((5, tn), F32))
    o_ref[...] = jnp.concatenate(rows, axis=0)


def _modulation(cond3, w_ada, b_ada):
    cb = jnp.broadcast_to(cond3[:, :, None], (3, D_MODEL, LANES))
    n = 6 * D_MODEL
    out = pl.pallas_call(
        _mod_kernel,
        grid=(DEPTH, n // MOD_TN),
        in_specs=[
            pl.BlockSpec((3, D_MODEL, LANES), lambda l, j: (0, 0, 0)),
            pl.BlockSpec((None, D_MODEL, MOD_TN), lambda l, j: (l, 0, j)),
            pl.BlockSpec((None, 1, MOD_TN), lambda l, j: (l, 0, j)),
        ],
        out_specs=pl.BlockSpec((None, 8, MOD_TN), lambda l, j: (l, 0, j)),
        out_shape=jax.ShapeDtypeStruct((DEPTH, 8, n), F32),
        scratch_shapes=[pltpu.VMEM((3, D_MODEL, LANES), F32)],
        compiler_params=_params(2),
        name="adaln_mod",
    )(cb, w_ada, b_ada.reshape(DEPTH, 1, n))
    return out[:, :3].reshape(DEPTH, 3, 6, D_MODEL)


IN_TM = 256


def _rope_full(a, c, s):
    return a * c + pltpu.roll(a, 64, 1) * s


def _rope_two_maps(a, c, s):
    lane = lax.broadcasted_iota(jnp.int32, a.shape, 1)
    partner = jnp.where((lane % 64) < 32, pltpu.roll(a, 96, 1), pltpu.roll(a, 32, 1))
    return a * c + partner * s


def _in_proj_kernel(latent, *refs):
    if latent:
        (x_ref, mod_ref, g_ref, w_ref, qkg_ref, gb_ref, c128_ref, s128_ref, c64_ref, s64_ref,
         q_ref, k_ref, v_ref, mq_ref, mk_ref, mv_ref, og_ref, dq_ref, dk_ref, dv_ref, gt_ref, gT_ref) = refs
    else:
        x_ref, mod_ref, g_ref, w_ref, qkg_ref, gb_ref = refs[:6]
        (q_ref, k_ref, v_ref, mq_ref, mk_ref, mv_ref, og_ref, dq_ref, dk_ref, dv_ref, gt_ref, gT_ref) = refs[-12:]

    x = x_ref[...]
    hn = _rms(x, g_ref[0:1, :]) * (1.0 + mod_ref[1:2, :]) + mod_ref[0:1, :]
    hnb = hn.astype(BF16)

    def proj(off, width):
        return _dot(hnb, w_ref[:, off:off + width])

    q_scale = HEAD_DIM ** -0.5 * LOG2E
    aq = proj(OFF_AQ, ATT_W)
    for h in range(ATT_HEADS):
        a = _rms(aq[:, h * HEAD_DIM:(h + 1) * HEAD_DIM], qkg_ref[0:1, :])
        if latent:
            a = _rope_full(a, c128_ref[...], s128_ref[...])
        q_ref[:, h * HEAD_DIM:(h + 1) * HEAD_DIM] = (a * q_scale).astype(q_ref.dtype)

    ak = proj(OFF_AK, KV_W)
    for h in range(ATT_KV_HEADS):
        a = _rms(ak[:, h * HEAD_DIM:(h + 1) * HEAD_DIM], qkg_ref[1:2, :])
        if latent:
            a = _rope_full(a, c128_ref[...], s128_ref[...])
        k_ref[:, h * HEAD_DIM:(h + 1) * HEAD_DIM] = a.astype(k_ref.dtype)

    v_ref[...] = proj(OFF_AV, KV_W).astype(v_ref.dtype)
    mq_ref[...] = proj(OFF_MQ, ML_W).astype(mq_ref.dtype)
    mk_ref[...] = (proj(OFF_MK, ML_W) * (HEAD_DIM ** -0.5)).astype(mk_ref.dtype)
    mv_ref[...] = proj(OFF_MV, ML_W).astype(mv_ref.dtype)
    og_ref[...] = _sigmoid(proj(OFF_MO, ML_W)).astype(og_ref.dtype)

    dq = proj(OFF_DQ, DF_W)
    dk = proj(OFF_DK, DF_W)
    dq_scale = DIFF_QK_DIM ** -0.5 * LOG2E
    for h in range(DIFF_HEADS):
        sl = slice(h * HEAD_DIM, (h + 1) * HEAD_DIM)
        a, b = dq[:, sl], dk[:, sl]
        if latent:
            a = _rope_two_maps(a, c64_ref[...], s64_ref[...])
            b = _rope_two_maps(b, c64_ref[...], s64_ref[...])
        dq_ref[:, sl] = (a * dq_scale).astype(dq_ref.dtype)
        dk_ref[:, sl] = b.astype(dk_ref.dtype)
    dv_ref[...] = proj(OFF_DV, DF_W).astype(dv_ref.dtype)

    g = proj(OFF_MG, LANES) + gb_ref[...]
    lane = lax.broadcasted_iota(jnp.int32, g.shape, 1)
    gates = jnp.where((lane & 4) != 0, _log_sigmoid(g), g)
    gt_ref[...] = gates
    for j in range(gates.shape[0] // CHUNK):
        gT_ref[j] = gates[j * CHUNK:(j + 1) * CHUNK, :].T


CACHE_OUTS = (1, 2, 8, 9)


def _in_proj(layer, x, mod_l, gains, w_in_p, qk_gain_l, gate_bias_row, rope, latent, caches=None):
    t = x.shape[0]
    tm = IN_TM
    n_tiles = t // tm
    per_seq = DEC_SEQ // tm
    if latent:
        cond_of = lambda i: 1 + i // per_seq
    else:
        cond_of = lambda i: 0
    row = lambda width: pl.BlockSpec((tm, width), lambda i: (i, 0))
    in_specs = [
        row(D_MODEL),
        pl.BlockSpec((None, 6, D_MODEL), lambda i: (cond_of(i), 0, 0)),
        pl.BlockSpec((4, D_MODEL), lambda i: (0, 0)),
        pl.BlockSpec((None, D_MODEL, IN_W_PACKED), lambda i: (layer, 0, 0), pipeline_mode=pl.Buffered(1)),
        pl.BlockSpec((2, HEAD_DIM), lambda i: (0, 0)),
        pl.BlockSpec((1, LANES), lambda i: (0, 0)),
    ]
    args = [x, mod_l, gains, w_in_p, qk_gain_l, gate_bias_row]
    if latent:
        in_specs += [pl.BlockSpec((tm, LANES), lambda i: (i % per_seq, 0))] * 4
        args += list(rope)
    kv_dt = BF16 if latent else F32
    widths_dtypes = [(ATT_W, BF16), (KV_W, kv_dt), (KV_W, kv_dt),
                     (ML_W, BF16), (ML_W, BF16), (ML_W, BF16), (ML_W, BF16),
                     (DF_W, BF16), (DF_W, kv_dt), (DF_W, kv_dt), (LANES, F32)]
    out_specs = [row(w) for w, _ in widths_dtypes] + [pl.BlockSpec((tm // CHUNK, LANES, CHUNK), lambda i: (i, 0, 0))]
    out_shape = ([jax.ShapeDtypeStruct((t, w), dt) for w, dt in widths_dtypes]
                 + [jax.ShapeDtypeStruct((t // CHUNK, LANES, CHUNK), F32)])
    aliases = {}
    if not latent:
        assert tm == SEQ
        for o in CACHE_OUTS:
            w = widths_dtypes[o][0]
            out_specs[o] = pl.BlockSpec((None, None, SEQ, w), lambda i: (i, layer, 0, 0))
            out_shape[o] = jax.ShapeDtypeStruct((BATCH, DEPTH, SEQ, w), F32)
        if caches is not None:
            aliases = {len(args) + j: o for j, o in enumerate(CACHE_OUTS)}
            in_specs += [pl.BlockSpec(memory_space=pl.ANY)] * len(CACHE_OUTS)
            args += list(caches)
    return pl.pallas_call(
        functools.partial(_in_proj_kernel, latent),
        grid=(n_tiles,),
        in_specs=in_specs,
        out_specs=out_specs,
        out_shape=out_shape,
        input_output_aliases=aliases,
        compiler_params=_params(1),
        name="in_proj_lat" if latent else "in_proj_ctx",
    )(*args)


ATT_SUB = 128


def _with_ones(v):
    return jnp.concatenate([v, jnp.ones_like(v)], axis=1)


def _attend_many(load_q, n, k, va, emit):
    scores, probs = {}, {}
    for t in range(n + 2):
        if t < n:
            scores[t] = _dot_nt(load_q(t), k)
        if 0 <= t - 1 < n:
            s = scores.pop(t - 1)
            probs[t - 1] = jnp.exp2(s - jnp.max(s, axis=-1, keepdims=True)).astype(BF16)
        if 0 <= t - 2 < n:
            oa = _dot(probs.pop(t - 2), va)
            emit(t - 2, oa[:, :HEAD_DIM] / oa[:, HEAD_DIM:])


def _diff_lambda(lam_ref, lam_init):
    lv = lam_ref[...]
    a = jnp.sum(lv[0:1, :] * lv[1:2, :], axis=-1, keepdims=True)
    b = jnp.sum(lv[2:3, :] * lv[3:4, :], axis=-1, keepdims=True)
    return jnp.exp(a) - jnp.exp(b) + lam_init


def _diff_attend_many(load_q, n, k, va, lam, gain, lam_init, emit):
    def load_map(c):
        q = load_q(c // 2)
        lane = lax.broadcasted_iota(jnp.int32, q.shape, 1)
        keep = (lane < DIFF_QK_DIM) if c % 2 == 0 else (lane >= DIFF_QK_DIM)
        return jnp.where(keep, q, jnp.zeros_like(q))

    first = {}

    def combine(c, o):
        if c % 2 == 0:
            first[c // 2] = o
        else:
            emit(c // 2, _rms(first.pop(c // 2) - lam * o, gain) * (1.0 - lam_init))

    _attend_many(load_map, 2 * n, k, va, combine)


def _tri(lower):
    r = lax.broadcasted_iota(jnp.int32, (CHUNK, CHUNK), 0)
    c = lax.broadcasted_iota(jnp.int32, (CHUNK, CHUNK), 1)
    return (c <= r) if lower else (c >= r)


def _scan_masks():
    masks = (_tri(True), _tri(False))
    return masks, tuple(jnp.where(mk, 1.0, 0.0).astype(BF16) for mk in masks)


def _split3(g):
    g1 = g.astype(BF16)
    r1 = g - g1.astype(F32)
    g2 = r1.astype(BF16)
    g3 = (r1 - g2.astype(F32)).astype(BF16)
    return g1, g2, g3


def _gate_cumsums(g_parts, gT_parts, tris, d):
    b = sum(_dot(tris[d], p) for p in g_parts)
    bT = sum(_dot(p, tris[1 - d]) for p in gT_parts)
    return b, bT


def _mlstm_chunks(items, masks):
    gate = []
    for (q, k, va, bcol, brow, icol, irow, d, CN, m) in items:
        bfull = jnp.broadcast_to(bcol, (CHUNK, CHUNK))
        ifull = jnp.broadcast_to(icol, (CHUNK, CHUNK))
        logw = jnp.where(masks[d], bfull - brow + irow, -jnp.inf)
        inter = bfull + m
        m_row = jnp.maximum(inter, jnp.max(logw, axis=1, keepdims=True))
        gate.append((bfull, ifull, jnp.exp(logw - m_row), jnp.exp(inter - m_row), m_row))
    first = [(_dot_nt(q, k), _dot(q, CN.astype(BF16))) for (q, k, va, *_, CN, m) in items]
    second = [_dot((qk * p).astype(BF16), it[2]) for it, (_, _, p, _, _), (qk, _) in zip(items, gate, first)]
    out = []
    for (q, k, va, bcol, brow, icol, irow, d, CN, m), (bfull, ifull, _, a_int, m_row), (_, qc), nd in zip(
            items, gate, first, second):
        num = nd[:, :HEAD_DIM] + a_int * qc[:, :HEAD_DIM]
        den = nd[:, HEAD_DIM:] + a_int * qc[:, HEAD_DIM:]
        h = num / jnp.maximum(jnp.abs(den), jnp.exp(-m_row))
        last = CHUNK - 1 if d == 0 else 0
        m_new = m_row[last:last + 1, :]
        btot = bfull[last:last + 1, :]
        w_end = jnp.exp(btot - bfull + ifull - m_new)
        decay = jnp.exp(btot + m - m_new)
        kw = (k.astype(F32) * w_end).astype(BF16)
        out.append((h, jnp.concatenate([decay, decay], axis=1) * CN + _dot_tn(kw, va), m_new))
    return out


def _gate_cols(d, h):
    return d * 8 + h, d * 8 + 4 + h


def _state_in(C, n_row):
    n_col = jnp.broadcast_to(n_row, (HEAD_DIM, HEAD_DIM)).T
    return jnp.concatenate([C, n_col], axis=1)


def _ctx_mix_kernel(lam_init, *refs):
    (q_ref, k_ref, v_ref, mq_ref, mk_ref, mv_ref, og_ref, dq_ref, dk_ref, dv_ref, gt_ref, gT_ref,
     mlg_ref, dfg_ref, lam_ref) = refs[:15]
    att_ref, ml_ref, dif_ref, C_ref, n_ref, m_ref = refs[-6:]
    n_tok = SEQ
    rep = ATT_HEADS // ATT_KV_HEADS
    for g in range(ATT_KV_HEADS):
        sl = slice(g * HEAD_DIM, (g + 1) * HEAD_DIM)
        kb = k_ref[:, sl].astype(BF16)
        va = _with_ones(v_ref[:, sl].astype(BF16))
        head = lambda r: slice((g * rep + r) * HEAD_DIM, (g * rep + r + 1) * HEAD_DIM)

        def put_att(r, o):
            att_ref[:, head(r)] = o.astype(att_ref.dtype)

        _attend_many(lambda r: q_ref[:, head(r)], rep, kb, va, put_att)

    lam = _diff_lambda(lam_ref, lam_init)
    for h in range(DIFF_HEADS):
        sl = slice(h * HEAD_DIM, (h + 1) * HEAD_DIM)

        def put_dif(_, d):
            dif_ref[:, sl] = d.astype(dif_ref.dtype)

        _diff_attend_many(lambda _: dq_ref[:, sl], 1, dk_ref[:, sl].astype(BF16),
                          _with_ones(dv_ref[:, sl].astype(BF16)), lam, dfg_ref[...], lam_init, put_dif)

    n_chunks = n_tok // CHUNK
    masks, tris = _scan_masks()
    forms = []
    for c in range(n_chunks):
        g = gt_ref[c * CHUNK:(c + 1) * CHUNK, :]
        gT = gT_ref[c]
        gp, gTp = _split3(g), _split3(gT)
        forms.append((g, gT) + tuple(_gate_cumsums(gp, gTp, tris, d) for d in range(2)))
    scans = [(d, h) for d in range(2) for h in range(MLSTM_HEADS)]
    state = [(jnp.zeros((HEAD_DIM, 2 * HEAD_DIM), F32), jnp.full((1, LANES), -jnp.inf, F32)) for _ in scans]
    hsum = {}
    for t in range(n_chunks):
        items = []
        for (d, h), (CN, m) in zip(scans, state):
            c = t if d == 0 else n_chunks - 1 - t
            g, gT = forms[c][0], forms[c][1]
            b, bT = forms[c][2 + d]
            ci, cf = _gate_cols(d, h)
            rows = slice(c * CHUNK, (c + 1) * CHUNK)
            sl = slice(h * HEAD_DIM, (h + 1) * HEAD_DIM)
            items.append((mq_ref[rows, sl], mk_ref[rows, sl], _with_ones(mv_ref[rows, sl]),
                          b[:, cf:cf + 1], bT[cf:cf + 1, :], g[:, ci:ci + 1], gT[ci:ci + 1, :], d, CN, m))
        res = _mlstm_chunks(items, masks)
        state = [(CN, m) for _, CN, m in res]
        for (d, h), (hc, _, _) in zip(scans, res):
            c = t if d == 0 else n_chunks - 1 - t
            hsum[(h, c)] = hc if (h, c) not in hsum else hsum[(h, c)] + hc
    for (d, h), (CN, m) in zip(scans, state):
        o = d * MLSTM_HEADS + h
        C_ref[d, h] = CN[:, :HEAD_DIM]
        n_ref[o:o + 1, :] = CN[:, HEAD_DIM:].T[0:1, :]
        m_ref[o:o + 1, :] = m
    for h in range(MLSTM_HEADS):
        sl = slice(h * HEAD_DIM, (h + 1) * HEAD_DIM)
        for c in range(n_chunks):
            rows = slice(c * CHUNK, (c + 1) * CHUNK)
            y = _rms(hsum[(h, c)], mlg_ref[h:h + 1, :]) * og_ref[rows, sl].astype(F32)
            ml_ref[rows, sl] = y.astype(ml_ref.dtype)


def _ctx_mix(layer, q, k, v, mq, mk, mv, og, dq, dk, dv, gt, gT, ml_gain, diff_gain, diff_lambda, states=None):
    lam_init = 0.8 - 0.6 * math.exp(-0.3 * layer)
    row = lambda width: pl.BlockSpec((SEQ, width), lambda b: (b, 0))
    cache = lambda width: pl.BlockSpec((None, None, SEQ, width), lambda b: (b, layer, 0, 0))
    full = lambda a: pl.BlockSpec(a.shape, lambda b: (0,) * a.ndim)
    nh = 2 * MLSTM_HEADS
    in_specs = [row(ATT_W), cache(KV_W), cache(KV_W), row(ML_W), row(ML_W), row(ML_W), row(ML_W),
                row(DF_W), cache(DF_W), cache(DF_W), row(LANES),
                pl.BlockSpec((SEQ // CHUNK, LANES, CHUNK), lambda b: (b, 0, 0)),
                full(ml_gain), full(diff_gain), full(diff_lambda)]
    args = [q, k, v, mq, mk, mv, og, dq, dk, dv, gt, gT, ml_gain, diff_gain, diff_lambda]
    aliases = {}
    if states is not None:
        aliases = {len(args) + j: 3 + j for j in range(3)}
        in_specs += [pl.BlockSpec(memory_space=pl.ANY)] * 3
        args += list(states)
    return pl.pallas_call(
        functools.partial(_ctx_mix_kernel, lam_init),
        grid=(BATCH,),
        in_specs=in_specs,
        out_specs=[row(ATT_W), row(ML_W), row(DF_W),
                   pl.BlockSpec((None, None, 2, MLSTM_HEADS, HEAD_DIM, HEAD_DIM), lambda b: (b, layer, 0, 0, 0, 0)),
                   pl.BlockSpec((None, None, nh, HEAD_DIM), lambda b: (b, layer, 0, 0)),
                   pl.BlockSpec((None, None, nh, LANES), lambda b: (b, layer, 0, 0))],
        out_shape=[jax.ShapeDtypeStruct((T_CTX, ATT_W), BF16),
                   jax.ShapeDtypeStruct((T_CTX, ML_W), BF16),
                   jax.ShapeDtypeStruct((T_CTX, DF_W), BF16),
                   jax.ShapeDtypeStruct((BATCH, DEPTH, 2, MLSTM_HEADS, HEAD_DIM, HEAD_DIM), F32),
                   jax.ShapeDtypeStruct((BATCH, DEPTH, nh, HEAD_DIM), F32),
                   jax.ShapeDtypeStruct((BATCH, DEPTH, nh, LANES), F32)],
        input_output_aliases=aliases,
        compiler_params=_params(1),
        name="ctx_mix",
    )(*args)


GQA_TQ = 512
DIFF_TQ = 1024


def _lat_gqa_kernel(q_ref, k_ref, v_ref, o_ref):
    blocks = [(slice(j * ATT_SUB, (j + 1) * ATT_SUB), slice(r * HEAD_DIM, (r + 1) * HEAD_DIM))
              for r in range(ATT_HEADS // ATT_KV_HEADS) for j in range(q_ref.shape[0] // ATT_SUB)]

    def put(i, o):
        o_ref[blocks[i]] = o.astype(o_ref.dtype)

    _attend_many(lambda i: q_ref[blocks[i]], len(blocks), k_ref[...], _with_ones(v_ref[...]), put)


def _lat_gqa(q, k_full, v_full):
    rep = ATT_HEADS // ATT_KV_HEADS
    nq = DEC_SEQ // GQA_TQ
    qspec = pl.BlockSpec((GQA_TQ, rep * HEAD_DIM), lambda b, g, i: (b * nq + i, g))
    kvspec = pl.BlockSpec((None, KEYS_LAT, HEAD_DIM), lambda b, g, i: (b, 0, g))
    return pl.pallas_call(
        _lat_gqa_kernel,
        grid=(DEC_BATCH, ATT_KV_HEADS, nq),
        in_specs=[qspec, kvspec, kvspec],
        out_specs=qspec,
        out_shape=jax.ShapeDtypeStruct((T_LAT, ATT_W), BF16),
        compiler_params=_params(3),
        name="lat_gqa",
    )(q, k_full, v_full)


def _lat_diff_kernel(lam_init, q_ref, k_ref, v_ref, dfg_ref, lam_ref, o_ref):
    lam = _diff_lambda(lam_ref, lam_init)
    rows = lambda j: slice(j * ATT_SUB, (j + 1) * ATT_SUB)

    def put(j, d):
        o_ref[rows(j), :] = d.astype(o_ref.dtype)

    _diff_attend_many(lambda j: q_ref[rows(j), :], q_ref.shape[0] // ATT_SUB, k_ref[...], _with_ones(v_ref[...]),
                      lam, dfg_ref[...], lam_init, put)


def _lat_diff(layer, dq, dk_full, dv_full, diff_gain, diff_lambda):
    lam_init = 0.8 - 0.6 * math.exp(-0.3 * layer)
    nq = DEC_SEQ // DIFF_TQ
    qspec = pl.BlockSpec((DIFF_TQ, HEAD_DIM), lambda b, h, i: (b * nq + i, h))
    kvspec = pl.BlockSpec((None, KEYS_LAT, HEAD_DIM), lambda b, h, i: (b, 0, h))
    full = lambda a: pl.BlockSpec(a.shape, lambda b, h, i: (0,) * a.ndim)
    return pl.pallas_call(
        functools.partial(_lat_diff_kernel, lam_init),
        grid=(DEC_BATCH, DIFF_HEADS, nq),
        in_specs=[qspec, kvspec, kvspec, full(diff_gain), full(diff_lambda)],
        out_specs=qspec,
        out_shape=jax.ShapeDtypeStruct((T_LAT, DF_W), BF16),
        compiler_params=_params(3),
        name="lat_diff",
    )(dq, dk_full, dv_full, diff_gain, diff_lambda)


def _lat_mlstm_kernel(q_ref, k_ref, v_ref, og_ref, gt_ref, gT_ref, C0_ref, n0_ref, m0_ref, mlg_ref,
                      o_ref, hs_ref):
    n_chunks = DEC_SEQ // CHUNK
    masks, tris = _scan_masks()

    def step(second_half, t, carry):
        items, where = [], []
        for d in range(2):
            c = t if d == 0 else n_chunks - 1 - t
            rows = pl.ds(pl.multiple_of(c * CHUNK, CHUNK), CHUNK)
            g = gt_ref[rows, :]
            gT = gT_ref[c]
            b, bT = _gate_cumsums(_split3(g), _split3(gT), tris, d)
            for h in range(MLSTM_HEADS):
                sl = slice(h * HEAD_DIM, (h + 1) * HEAD_DIM)
                ci, cf = _gate_cols(d, h)
                CN, m = carry[d * MLSTM_HEADS + h]
                items.append((q_ref[rows, sl], k_ref[rows, sl], _with_ones(v_ref[rows, sl]),
                              b[:, cf:cf + 1], bT[cf:cf + 1, :], g[:, ci:ci + 1], gT[ci:ci + 1, :], d, CN, m))
                where.append((rows, sl, h))
        res = _mlstm_chunks(items, masks)
        for (rows, sl, h), (hc, _, _) in zip(where, res):
            if second_half:
                y = _rms(hs_ref[rows, sl] + hc, mlg_ref[h:h + 1, :]) * og_ref[rows, sl].astype(F32)
                o_ref[rows, sl] = y.astype(o_ref.dtype)
            else:
                hs_ref[rows, sl] = hc
        return tuple((CN, m) for _, CN, m in res)

    init = tuple((_state_in(C0_ref[d, h], n0_ref[d, h]), m0_ref[d, h])
                 for d in range(2) for h in range(MLSTM_HEADS))
    half = n_chunks // 2
    carry = lax.fori_loop(0, half, functools.partial(step, False), init)
    lax.fori_loop(half, n_chunks, functools.partial(step, True), carry)


def _lat_mlstm(layer, mq, mk, mv, og, gt, gT, state_C, state_n6, state_m6, ml_gain):
    once = pl.Buffered(1)
    hspec = pl.BlockSpec((DEC_SEQ, ML_W), lambda b: (b, 0), pipeline_mode=once)
    st = lambda last2: pl.BlockSpec((None, None, 2, MLSTM_HEADS) + last2, lambda b: (b, layer, 0, 0, 0, 0))
    return pl.pallas_call(
        _lat_mlstm_kernel,
        grid=(DEC_BATCH,),
        in_specs=[hspec, hspec, hspec, hspec,
                  pl.BlockSpec((DEC_SEQ, LANES), lambda b: (b, 0), pipeline_mode=once),
                  pl.BlockSpec((DEC_SEQ // CHUNK, LANES, CHUNK), lambda b: (b, 0, 0), pipeline_mode=once),
                  st((HEAD_DIM, HEAD_DIM)), st((1, HEAD_DIM)), st((1, LANES)),
                  pl.BlockSpec(ml_gain.shape, lambda b: (0, 0))],
        out_specs=pl.BlockSpec((DEC_SEQ, ML_W), lambda b: (b, 0)),
        out_shape=jax.ShapeDtypeStruct((T_LAT, ML_W), BF16),
        scratch_shapes=[pltpu.VMEM((DEC_SEQ, ML_W), F32)],
        compiler_params=_params(1),
        name="lat_mlstm",
    )(mq, mk, mv, og, gt, gT, state_C, state_n6, state_m6, ml_gain)


OUT_TM = 512
FFN_TM = 512
FFN_TF = 1024
ROW_BLOCK = 16


def _cond_index(latent, tm):
    per_seq = DEC_SEQ // tm
    if latent:
        return lambda i: 1 + i // per_seq
    return lambda i: 0


def _out_proj_kernel(att_ref, ml_ref, dif_ref, w_ref, x_ref, mod_ref, g_ref, o_ref):
    mix = (_dot(att_ref[...], w_ref[0:ATT_W, :])
           + _dot(ml_ref[...], w_ref[ATT_W:ATT_W + ML_W, :])
           + _dot(dif_ref[...], w_ref[ATT_W + ML_W:, :]))
    o_ref[...] = x_ref[...] + mod_ref[2:3, :] * _rms(mix, g_ref[1:2, :])


def _out_proj(layer, att, ml, dif, w_out_b, x, mod_l, gains, latent):
    t = x.shape[0]
    tm = OUT_TM
    cond_of = _cond_index(latent, tm)
    row = lambda width: pl.BlockSpec((tm, width), lambda i: (i, 0))
    return pl.pallas_call(
        _out_proj_kernel,
        grid=(t // tm,),
        in_specs=[row(ATT_W), row(ML_W), row(DF_W),
                  pl.BlockSpec((None, D_MODEL, D_MODEL), lambda i: (layer, 0, 0), pipeline_mode=pl.Buffered(1)),
                  row(D_MODEL),
                  pl.BlockSpec((None, 6, D_MODEL), lambda i: (cond_of(i), 0, 0)),
                  pl.BlockSpec((4, D_MODEL), lambda i: (0, 0))],
        out_specs=row(D_MODEL),
        out_shape=jax.ShapeDtypeStruct((t, D_MODEL), F32),
        compiler_params=_params(1),
        name="out_proj",
    )(att, ml, dif, w_out_b, x, mod_l, gains)


def _ffn_kernel(x_ref, mod_ref, g_ref, w1_ref, w2_ref, o_ref, hn_ref):
    f = pl.program_id(1)

    def row_blocks(body):
        for i in range(x_ref.shape[0] // ROW_BLOCK):
            body(slice(i * ROW_BLOCK, (i + 1) * ROW_BLOCK))

    @pl.when(f == 0)
    def _():
        def body(rows):
            hn = _rms(x_ref[rows, :], g_ref[2:3, :]) * (1.0 + mod_ref[4:5, :]) + mod_ref[3:4, :]
            hn_ref[rows, :] = hn.astype(hn_ref.dtype)
            o_ref[rows, :] = jnp.zeros((ROW_BLOCK, o_ref.shape[1]), o_ref.dtype)
        row_blocks(body)

    h = jnp.maximum(_dot(hn_ref[...], w1_ref[...]), 0.0)
    o_ref[...] += _dot((h * h).astype(BF16), w2_ref[...])

    @pl.when(f == pl.num_programs(1) - 1)
    def _():
        def body(rows):
            o_ref[rows, :] = x_ref[rows, :] + mod_ref[5:6, :] * _rms(o_ref[rows, :], g_ref[3:4, :])
        row_blocks(body)


def _ffn(layer, x, mod_l, gains, w1_b, w2_b, latent):
    t = x.shape[0]
    tm, tf = FFN_TM, FFN_TF
    cond_of = _cond_index(latent, tm)
    return pl.pallas_call(
        _ffn_kernel,
        grid=(t // tm, D_FF // tf),
        in_specs=[pl.BlockSpec((tm, D_MODEL), lambda i, f: (i, 0)),
                  pl.BlockSpec((None, 6, D_MODEL), lambda i, f: (cond_of(i), 0, 0)),
                  pl.BlockSpec((4, D_MODEL), lambda i, f: (0, 0)),
                  pl.BlockSpec((None, D_MODEL, tf), lambda i, f: (layer, 0, f)),
                  pl.BlockSpec((None, tf, D_MODEL), lambda i, f: (layer, f, 0))],
        out_specs=pl.BlockSpec((tm, D_MODEL), lambda i, f: (i, 0)),
        out_shape=jax.ShapeDtypeStruct((t, D_MODEL), F32),
        scratch_shapes=[pltpu.VMEM((tm, D_MODEL), BF16)],
        compiler_params=_params(2),
        name="ffn",
    )(x, mod_l, gains, w1_b, w2_b)


def _rope_tables():
    def angles(dim):
        rows = DEC_SEQ // GRID_W
        row_idx = jnp.repeat(jnp.arange(rows), GRID_W).astype(F32)
        col_idx = jnp.tile(jnp.arange(GRID_W), rows).astype(F32)
        n_freq = dim // 4
        inv = ROPE_THETA ** (-jnp.arange(n_freq, dtype=F32) / n_freq)
        ang = jnp.concatenate([row_idx[:, None] * inv, col_idx[:, None] * inv], axis=-1)
        return jnp.cos(ang), jnp.sin(ang)

    c, s = angles(HEAD_DIM)
    c2, s2 = angles(DIFF_QK_DIM)
    c128 = jnp.concatenate([c, c], axis=-1)
    s128 = jnp.concatenate([-s, s], axis=-1)
    c64 = jnp.concatenate([c2, c2, c2, c2], axis=-1)
    s64 = jnp.concatenate([-s2, s2, -s2, s2], axis=-1)
    return c128, s128, c64, s64


PACK_TK = 256
GATE_COL0 = OFF_MO + ML_W


def _pack_kernel(w_ref, o_ref):
    o_ref[:, :GATE_COL0] = w_ref[:, :GATE_COL0].astype(BF16)
    o_ref[:, GATE_COL0:OFF_MG] = w_ref[:, GATE_COL0 + N_GATES:].astype(BF16)
    pad = jnp.zeros((w_ref.shape[0], LANES - N_GATES), F32)
    o_ref[:, OFF_MG:] = jnp.concatenate([w_ref[:, GATE_COL0:GATE_COL0 + N_GATES], pad], axis=1).astype(BF16)


def _pack_w_in(w_in):
    in_w = w_in.shape[-1]
    return pl.pallas_call(
        _pack_kernel,
        grid=(DEPTH, D_MODEL // PACK_TK),
        in_specs=[pl.BlockSpec((None, PACK_TK, in_w), lambda l, i: (l, i, 0))],
        out_specs=pl.BlockSpec((None, PACK_TK, IN_W_PACKED), lambda l, i: (l, i, 0)),
        out_shape=jax.ShapeDtypeStruct((DEPTH, D_MODEL, IN_W_PACKED), BF16),
        compiler_params=_params(2),
        name="pack_w_in",
    )(w_in)


def kernel(x_prompt, x_sample, c, cache_gqa_k, cache_gqa_v, cache_diff_k, cache_diff_v, state_mlstm_C,
           state_mlstm_n, state_mlstm_m, c_ctx, w_ada, b_ada, norm_gain, w_in, w_out, qk_gain,
           mlstm_gate_bias, mlstm_head_gain, diff_lambda, diff_head_gain, w_ff1, w_ff2):
    cond3 = jnp.concatenate([c_ctx[None, :], c], axis=0)
    mod = _modulation(cond3, w_ada, b_ada)

    w_in_p = _pack_w_in(w_in)
    w_out_b = w_out.astype(BF16)
    w1_b = w_ff1.astype(BF16)
    w2_b = w_ff2.astype(BF16)
    gate_bias = jnp.pad(mlstm_gate_bias.reshape(DEPTH, 1, N_GATES), ((0, 0), (0, 0), (0, LANES - N_GATES)))
    diff_gain = diff_head_gain.reshape(DEPTH, 1, HEAD_DIM)
    rope = _rope_tables()

    state_n6 = state_mlstm_n.reshape(DEC_BATCH, DEPTH, 2, MLSTM_HEADS, 1, HEAD_DIM)
    state_m6 = jnp.broadcast_to(state_mlstm_m[..., None, None], (DEC_BATCH, DEPTH, 2, MLSTM_HEADS, 1, LANES))

    xp = x_prompt.reshape(T_CTX, D_MODEL)
    xs = x_sample.reshape(T_LAT, D_MODEL)
    caches = None
    states = None

    for l in range(DEPTH):
        (q, k, v, mq, mk, mv, og, dq, dk, dv, gt, gT) = _in_proj(
            l, xp, mod[l], norm_gain[l], w_in_p, qk_gain[l], gate_bias[l], None, False, caches)
        caches = (k, v, dk, dv)
        att, ml, dif, *states = _ctx_mix(l, q, k, v, mq, mk, mv, og, dq, dk, dv, gt, gT,
                                         mlstm_head_gain[l], diff_gain[l], diff_lambda[l], states)
        xp = _out_proj(l, att, ml, dif, w_out_b, xp, mod[l], norm_gain[l], False)
        xp = _ffn(l, xp, mod[l], norm_gain[l], w1_b, w2_b, False)

        (q, k, v, mq, mk, mv, og, dq, dk, dv, gt, gT) = _in_proj(
            l, xs, mod[l], norm_gain[l], w_in_p, qk_gain[l], gate_bias[l], rope, True)
        as_keys = lambda new_rows, cache, width: jnp.concatenate(
            [new_rows.reshape(DEC_BATCH, DEC_SEQ, width),
             cache[:, l].reshape(DEC_BATCH, PAST_LEN, width).astype(BF16)], axis=1)
        att = _lat_gqa(q, as_keys(k, cache_gqa_k, KV_W), as_keys(v, cache_gqa_v, KV_W))
        dif = _lat_diff(l, dq, as_keys(dk, cache_diff_k, DF_W), as_keys(dv, cache_diff_v, DF_W),
                        diff_gain[l], diff_lambda[l])
        ml = _lat_mlstm(l, mq, mk, mv, og, gt, gT, state_mlstm_C, state_n6, state_m6, mlstm_head_gain[l])
        xs = _out_proj(l, att, ml, dif, w_out_b, xs, mod[l], norm_gain[l], True)
        xs = _ffn(l, xs, mod[l], norm_gain[l], w1_b, w2_b, True)

    new_k, new_v, new_dk, new_dv = caches
    new_C, new_n, new_m = states
    return (xp.reshape(BATCH, SEQ, D_MODEL), xs.reshape(DEC_BATCH, DEC_SEQ, D_MODEL),
            new_k.reshape(BATCH, DEPTH, SEQ, ATT_KV_HEADS, HEAD_DIM),
            new_v.reshape(BATCH, DEPTH, SEQ, ATT_KV_HEADS, HEAD_DIM),
            new_dk.reshape(BATCH, DEPTH, SEQ, DIFF_HEADS, HEAD_DIM),
            new_dv.reshape(BATCH, DEPTH, SEQ, DIFF_HEADS, HEAD_DIM),
            new_C,
            new_n.reshape(BATCH, DEPTH, 2, MLSTM_HEADS, HEAD_DIM),
            new_m[..., 0].reshape(BATCH, DEPTH, 2, MLSTM_HEADS))
```

```python
import functools
import math

import jax
import jax.numpy as jnp
from jax import lax
from jax.experimental import pallas as pl
from jax.experimental.pallas import tpu as pltpu

F32 = jnp.float32
BF16 = jnp.bfloat16

D_MODEL = 2048
BATCH = 32
SEQ = 256
DEPTH = 4
DEC_BATCH = 2
DEC_SEQ = 4096
PAST_LEN = 256
GRID_W = 64
HEAD_DIM = 128
ATT_W = 1024
ML_W = 512
DF_W = 512
ATT_HEADS = 8
ATT_KV_HEADS = 2
KV_W = 256
MLSTM_HEADS = 4
DIFF_HEADS = 4
DIFF_QK_DIM = 64
N_GATES = 16
D_FF = 8192
CHUNK = 128
ROPE_THETA = 10000.0
EPS = 1e-6
LOG2E = 1.4426950408889634
LANES = 128

T_CTX = BATCH * SEQ
T_LAT = DEC_BATCH * DEC_SEQ
KEYS_LAT = DEC_SEQ + PAST_LEN

OFF_AQ, OFF_AK, OFF_AV = 0, 1024, 1280
OFF_MQ, OFF_MK, OFF_MV, OFF_MO = 1536, 2048, 2560, 3072
OFF_MG = 3584
OFF_DQ, OFF_DK, OFF_DV = 3600, 4112, 4624
IN_W = 5136

VMEM_LIMIT = 52 * 1024 * 1024


def _dot(a, b):
    return jnp.dot(a, b, preferred_element_type=F32)


def _dot_nt(a, b):
    return lax.dot_general(a, b, (((1,), (1,)), ((), ())), preferred_element_type=F32)


def _dot_tn(a, b):
    return lax.dot_general(a, b, (((0,), (0,)), ((), ())), preferred_element_type=F32)


def _rms(x, gain):
    ms = jnp.mean(x * x, axis=-1, keepdims=True)
    return x * lax.rsqrt(ms + EPS) * gain


def _sigmoid(x):
    return 1.0 / (1.0 + jnp.exp(-x))


def _log_sigmoid(x):
    return jnp.minimum(x, 0.0) - jnp.log(1.0 + jnp.exp(-jnp.abs(x)))


def _params(n_axes):
    return pltpu.CompilerParams(dimension_semantics=("arbitrary",) * n_axes,
                                vmem_limit_bytes=VMEM_LIMIT)


MOD_TN = 1024
MOD_KC = 64


def _mod_kernel(c_ref, w_ref, b_ref, o_ref, s_ref):
    @pl.when((pl.program_id(0) == 0) & (pl.program_id(1) == 0))
    def _():
        cb = c_ref[...]
        s_ref[...] = cb * _sigmoid(cb)

    tn = w_ref.shape[1]
    reps = tn // LANES

    def body(g, accs):
        rows = pl.ds(pl.multiple_of(g * MOD_KC, MOD_KC), MOD_KC)
        wk = w_ref[rows, :].reshape(MOD_KC // 8, 8, tn)
        out = []
        for r in range(3):
            sk = s_ref[r, rows, :].reshape(MOD_KC // 8, 8, LANES)
            out.append(accs[r] + jnp.sum(wk * jnp.concatenate([sk] * reps, axis=2), axis=0))
        return tuple(out)

    zero = jnp.zeros((8, tn), F32)
    accs = lax.fori_loop(0, D_MODEL // MOD_KC, body, (zero, zero, zero))
    rows = [jnp.sum(a, axis=0, keepdims=True) + b_ref[...] for a in accs]
    rows.append(jnp.zeros((5, tn), F32))
    o_ref[...] = jnp.concatenate(rows, axis=0)


def _modulation(cond3, w_ada, b_ada):
    cb = jnp.broadcast_to(cond3[:, :, None], (3, D_MODEL, LANES))
    n = 6 * D_MODEL
    out = pl.pallas_call(
        _mod_kernel,
        grid=(DEPTH, n // MOD_TN),
        in_specs=[
            pl.BlockSpec((3, D_MODEL, LANES), lambda l, j: (0, 0, 0)),
            pl.BlockSpec((None, D_MODEL, MOD_TN), lambda l, j: (l, 0, j)),
            pl.BlockSpec((None, 1, MOD_TN), lambda l, j: (l, 0, j)),
        ],
        out_specs=pl.BlockSpec((None, 8, MOD_TN), lambda l, j: (l, 0, j)),
        out_shape=jax.ShapeDtypeStruct((DEPTH, 8, n), F32),
        scratch_shapes=[pltpu.VMEM((3, D_MODEL, LANES), F32)],
        compiler_params=_params(2),
        name="adaln_mod",
    )(cb, w_ada, b_ada.reshape(DEPTH, 1, n))
    return out[:, :3].reshape(DEPTH, 3, 6, D_MODEL)


IN_TM = 256


def _rope_full(a, c, s):
    return a * c + pltpu.roll(a, 64, 1) * s


def _rope_two_maps(a, c, s):
    lane = lax.broadcasted_iota(jnp.int32, a.shape, 1)
    partner = jnp.where((lane % 64) < 32, pltpu.roll(a, 96, 1), pltpu.roll(a, 32, 1))
    return a * c + partner * s


def _in_proj_kernel(latent, *refs):
    if latent:
        (x_ref, mod_ref, g_ref, w_ref, qkg_ref, gb_ref, c128_ref, s128_ref, c64_ref, s64_ref,
         q_ref, k_ref, v_ref, mq_ref, mk_ref, mv_ref, og_ref, dq_ref, dk_ref, dv_ref, gt_ref, gT_ref) = refs
    else:
        x_ref, mod_ref, g_ref, w_ref, qkg_ref, gb_ref = refs[:6]
        (q_ref, k_ref, v_ref, mq_ref, mk_ref, mv_ref, og_ref, dq_ref, dk_ref, dv_ref, gt_ref, gT_ref) = refs[-12:]

    x = x_ref[...]
    hn = _rms(x, g_ref[0:1, :]) * (1.0 + mod_ref[1:2, :]) + mod_ref[0:1, :]
    hnb = hn.astype(BF16)

    def proj(off, width):
        return _dot_nt(hnb, w_ref[off:off + width, :])

    q_scale = HEAD_DIM ** -0.5 * LOG2E
    aq = proj(OFF_AQ, ATT_W)
    for h in range(ATT_HEADS):
        a = _rms(aq[:, h * HEAD_DIM:(h + 1) * HEAD_DIM], qkg_ref[0:1, :])
        if latent:
            a = _rope_full(a, c128_ref[...], s128_ref[...])
        q_ref[:, h * HEAD_DIM:(h + 1) * HEAD_DIM] = (a * q_scale).astype(q_ref.dtype)

    ak = proj(OFF_AK, KV_W)
    for h in range(ATT_KV_HEADS):
        a = _rms(ak[:, h * HEAD_DIM:(h + 1) * HEAD_DIM], qkg_ref[1:2, :])
        if latent:
            a = _rope_full(a, c128_ref[...], s128_ref[...])
        k_ref[:, h * HEAD_DIM:(h + 1) * HEAD_DIM] = a.astype(k_ref.dtype)

    v_ref[...] = proj(OFF_AV, KV_W).astype(v_ref.dtype)
    mq_ref[...] = proj(OFF_MQ, ML_W).astype(mq_ref.dtype)
    mk_ref[...] = (proj(OFF_MK, ML_W) * (HEAD_DIM ** -0.5)).astype(mk_ref.dtype)
    mv_ref[...] = proj(OFF_MV, ML_W).astype(mv_ref.dtype)
    og_ref[...] = _sigmoid(proj(OFF_MO, ML_W)).astype(og_ref.dtype)

    dq = proj(OFF_DQ, DF_W)
    dk = proj(OFF_DK, DF_W)
    dq_scale = DIFF_QK_DIM ** -0.5 * LOG2E
    for h in range(DIFF_HEADS):
        sl = slice(h * HEAD_DIM, (h + 1) * HEAD_DIM)
        a, b = dq[:, sl], dk[:, sl]
        if latent:
            a = _rope_two_maps(a, c64_ref[...], s64_ref[...])
            b = _rope_two_maps(b, c64_ref[...], s64_ref[...])
        dq_ref[:, sl] = (a * dq_scale).astype(dq_ref.dtype)
        dk_ref[:, sl] = b.astype(dk_ref.dtype)
    dv_ref[...] = proj(OFF_DV, DF_W).astype(dv_ref.dtype)

    g = proj(OFF_MG, LANES) + gb_ref[...]
    lane = lax.broadcasted_iota(jnp.int32, g.shape, 1)
    gates = jnp.where((lane & 4) != 0, _log_sigmoid(g), g)
    gt_ref[...] = gates
    for j in range(gates.shape[0] // CHUNK):
        gT_ref[j] = gates[j * CHUNK:(j + 1) * CHUNK, :].T


CACHE_OUTS = (1, 2, 8, 9)


def _in_proj(layer, x, mod_l, gains, w_in_t, qk_gain_l, gate_bias_row, rope, latent, caches=None):
    t = x.shape[0]
    tm = IN_TM
    n_tiles = t // tm
    per_seq = DEC_SEQ // tm
    if latent:
        cond_of = lambda i: 1 + i // per_seq
    else:
        cond_of = lambda i: 0
    row = lambda width: pl.BlockSpec((tm, width), lambda i: (i, 0))
    in_specs = [
        row(D_MODEL),
        pl.BlockSpec((None, 6, D_MODEL), lambda i: (cond_of(i), 0, 0)),
        pl.BlockSpec((4, D_MODEL), lambda i: (0, 0)),
        pl.BlockSpec((None, IN_W, D_MODEL), lambda i: (layer, 0, 0), pipeline_mode=pl.Buffered(1)),
        pl.BlockSpec((2, HEAD_DIM), lambda i: (0, 0)),
        pl.BlockSpec((1, LANES), lambda i: (0, 0)),
    ]
    args = [x, mod_l, gains, w_in_t, qk_gain_l, gate_bias_row]
    if latent:
        in_specs += [pl.BlockSpec((tm, LANES), lambda i: (i % per_seq, 0))] * 4
        args += list(rope)
    kv_dt = BF16 if latent else F32
    widths_dtypes = [(ATT_W, BF16), (KV_W, kv_dt), (KV_W, kv_dt),
                     (ML_W, BF16), (ML_W, BF16), (ML_W, BF16), (ML_W, BF16),
                     (DF_W, BF16), (DF_W, kv_dt), (DF_W, kv_dt), (LANES, F32)]
    out_specs = [row(w) for w, _ in widths_dtypes] + [pl.BlockSpec((tm // CHUNK, LANES, CHUNK), lambda i: (i, 0, 0))]
    out_shape = ([jax.ShapeDtypeStruct((t, w), dt) for w, dt in widths_dtypes]
                 + [jax.ShapeDtypeStruct((t // CHUNK, LANES, CHUNK), F32)])
    aliases = {}
    if not latent:
        assert tm == SEQ
        for o in CACHE_OUTS:
            w = widths_dtypes[o][0]
            out_specs[o] = pl.BlockSpec((None, None, SEQ, w), lambda i: (i, layer, 0, 0))
            out_shape[o] = jax.ShapeDtypeStruct((BATCH, DEPTH, SEQ, w), F32)
        if caches is not None:
            aliases = {len(args) + j: o for j, o in enumerate(CACHE_OUTS)}
            in_specs += [pl.BlockSpec(memory_space=pl.ANY)] * len(CACHE_OUTS)
            args += list(caches)
    return pl.pallas_call(
        functools.partial(_in_proj_kernel, latent),
        grid=(n_tiles,),
        in_specs=in_specs,
        out_specs=out_specs,
        out_shape=out_shape,
        input_output_aliases=aliases,
        compiler_params=_params(1),
        name="in_proj_lat" if latent else "in_proj_ctx",
    )(*args)


ATT_SUB = 128


def _with_ones(v):
    return jnp.concatenate([v, jnp.ones_like(v)], axis=1)


def _attend_many(load_q, n, k, va, emit):
    scores, probs = {}, {}
    for t in range(n + 2):
        if t < n:
            scores[t] = _dot_nt(load_q(t), k)
        if 0 <= t - 1 < n:
            s = scores.pop(t - 1)
            probs[t - 1] = jnp.exp2(s - jnp.max(s, axis=-1, keepdims=True)).astype(BF16)
        if 0 <= t - 2 < n:
            oa = _dot(probs.pop(t - 2), va)
            emit(t - 2, oa[:, :HEAD_DIM] / oa[:, HEAD_DIM:])


def _diff_lambda(lam_ref, lam_init):
    lv = lam_ref[...]
    a = jnp.sum(lv[0:1, :] * lv[1:2, :], axis=-1, keepdims=True)
    b = jnp.sum(lv[2:3, :] * lv[3:4, :], axis=-1, keepdims=True)
    return jnp.exp(a) - jnp.exp(b) + lam_init


def _diff_attend_many(load_q, n, k, va, lam, gain, lam_init, emit):
    def load_map(c):
        q = load_q(c // 2)
        lane = lax.broadcasted_iota(jnp.int32, q.shape, 1)
        keep = (lane < DIFF_QK_DIM) if c % 2 == 0 else (lane >= DIFF_QK_DIM)
        return jnp.where(keep, q, jnp.zeros_like(q))

    first = {}

    def combine(c, o):
        if c % 2 == 0:
            first[c // 2] = o
        else:
            emit(c // 2, _rms(first.pop(c // 2) - lam * o, gain) * (1.0 - lam_init))

    _attend_many(load_map, 2 * n, k, va, combine)


def _tri(lower):
    r = lax.broadcasted_iota(jnp.int32, (CHUNK, CHUNK), 0)
    c = lax.broadcasted_iota(jnp.int32, (CHUNK, CHUNK), 1)
    return (c <= r) if lower else (c >= r)


def _scan_masks():
    masks = (_tri(True), _tri(False))
    return masks, tuple(jnp.where(mk, 1.0, 0.0).astype(BF16) for mk in masks)


def _split3(g):
    g1 = g.astype(BF16)
    r1 = g - g1.astype(F32)
    g2 = r1.astype(BF16)
    g3 = (r1 - g2.astype(F32)).astype(BF16)
    return g1, g2, g3


def _gate_cumsums(g_parts, gT_parts, tris, d):
    b = sum(_dot(tris[d], p) for p in g_parts)
    bT = sum(_dot(p, tris[1 - d]) for p in gT_parts)
    return b, bT


def _mlstm_chunks(items, masks):
    gate = []
    for (q, k, va, bcol, brow, icol, irow, d, CN, m) in items:
        bfull = jnp.broadcast_to(bcol, (CHUNK, CHUNK))
        ifull = jnp.broadcast_to(icol, (CHUNK, CHUNK))
        logw = jnp.where(masks[d], bfull - brow + irow, -jnp.inf)
        inter = bfull + m
        m_row = jnp.maximum(inter, jnp.max(logw, axis=1, keepdims=True))
        gate.append((bfull, ifull, jnp.exp(logw - m_row), jnp.exp(inter - m_row), m_row))
    first = [(_dot_nt(q, k), _dot(q, CN.astype(BF16))) for (q, k, va, *_, CN, m) in items]
    second = [_dot((qk * p).astype(BF16), it[2]) for it, (_, _, p, _, _), (qk, _) in zip(items, gate, first)]
    out = []
    for (q, k, va, bcol, brow, icol, irow, d, CN, m), (bfull, ifull, _, a_int, m_row), (_, qc), nd in zip(
            items, gate, first, second):
        num = nd[:, :HEAD_DIM] + a_int * qc[:, :HEAD_DIM]
        den = nd[:, HEAD_DIM:] + a_int * qc[:, HEAD_DIM:]
        h = num / jnp.maximum(jnp.abs(den), jnp.exp(-m_row))
        last = CHUNK - 1 if d == 0 else 0
        m_new = m_row[last:last + 1, :]
        btot = bfull[last:last + 1, :]
        w_end = jnp.exp(btot - bfull + ifull - m_new)
        decay = jnp.exp(btot + m - m_new)
        kw = (k.astype(F32) * w_end).astype(BF16)
        out.append((h, jnp.concatenate([decay, decay], axis=1) * CN + _dot_tn(kw, va), m_new))
    return out


def _gate_cols(d, h):
    return d * 8 + h, d * 8 + 4 + h


def _state_in(C, n_row):
    n_col = jnp.broadcast_to(n_row, (HEAD_DIM, HEAD_DIM)).T
    return jnp.concatenate([C, n_col], axis=1)


def _ctx_mix_kernel(lam_init, *refs):
    (q_ref, k_ref, v_ref, mq_ref, mk_ref, mv_ref, og_ref, dq_ref, dk_ref, dv_ref, gt_ref, gT_ref,
     mlg_ref, dfg_ref, lam_ref) = refs[:15]
    att_ref, ml_ref, dif_ref, C_ref, n_ref, m_ref = refs[-6:]
    n_tok = SEQ
    rep = ATT_HEADS // ATT_KV_HEADS
    for g in range(ATT_KV_HEADS):
        sl = slice(g * HEAD_DIM, (g + 1) * HEAD_DIM)
        kb = k_ref[:, sl].astype(BF16)
        va = _with_ones(v_ref[:, sl].astype(BF16))
        head = lambda r: slice((g * rep + r) * HEAD_DIM, (g * rep + r + 1) * HEAD_DIM)

        def put_att(r, o):
            att_ref[:, head(r)] = o.astype(att_ref.dtype)

        _attend_many(lambda r: q_ref[:, head(r)], rep, kb, va, put_att)

    lam = _diff_lambda(lam_ref, lam_init)
    for h in range(DIFF_HEADS):
        sl = slice(h * HEAD_DIM, (h + 1) * HEAD_DIM)

        def put_dif(_, d):
            dif_ref[:, sl] = d.astype(dif_ref.dtype)

        _diff_attend_many(lambda _: dq_ref[:, sl], 1, dk_ref[:, sl].astype(BF16),
                          _with_ones(dv_ref[:, sl].astype(BF16)), lam, dfg_ref[...], lam_init, put_dif)

    n_chunks = n_tok // CHUNK
    masks, tris = _scan_masks()
    forms = []
    for c in range(n_chunks):
        g = gt_ref[c * CHUNK:(c + 1) * CHUNK, :]
        gT = gT_ref[c]
        gp, gTp = _split3(g), _split3(gT)
        forms.append((g, gT) + tuple(_gate_cumsums(gp, gTp, tris, d) for d in range(2)))
    scans = [(d, h) for d in range(2) for h in range(MLSTM_HEADS)]
    state = [(jnp.zeros((HEAD_DIM, 2 * HEAD_DIM), F32), jnp.full((1, LANES), -jnp.inf, F32)) for _ in scans]
    hsum = {}
    for t in range(n_chunks):
        items = []
        for (d, h), (CN, m) in zip(scans, state):
            c = t if d == 0 else n_chunks - 1 - t
            g, gT = forms[c][0], forms[c][1]
            b, bT = forms[c][2 + d]
            ci, cf = _gate_cols(d, h)
            rows = slice(c * CHUNK, (c + 1) * CHUNK)
            sl = slice(h * HEAD_DIM, (h + 1) * HEAD_DIM)
            items.append((mq_ref[rows, sl], mk_ref[rows, sl], _with_ones(mv_ref[rows, sl]),
                          b[:, cf:cf + 1], bT[cf:cf + 1, :], g[:, ci:ci + 1], gT[ci:ci + 1, :], d, CN, m))
        res = _mlstm_chunks(items, masks)
        state = [(CN, m) for _, CN, m in res]
        for (d, h), (hc, _, _) in zip(scans, res):
            c = t if d == 0 else n_chunks - 1 - t
            hsum[(h, c)] = hc if (h, c) not in hsum else hsum[(h, c)] + hc
    for (d, h), (CN, m) in zip(scans, state):
        o = d * MLSTM_HEADS + h
        C_ref[d, h] = CN[:, :HEAD_DIM]
        n_ref[o:o + 1, :] = CN[:, HEAD_DIM:].T[0:1, :]
        m_ref[o:o + 1, :] = m
    for h in range(MLSTM_HEADS):
        sl = slice(h * HEAD_DIM, (h + 1) * HEAD_DIM)
        for c in range(n_chunks):
            rows = slice(c * CHUNK, (c + 1) * CHUNK)
            y = _rms(hsum[(h, c)], mlg_ref[h:h + 1, :]) * og_ref[rows, sl].astype(F32)
            ml_ref[rows, sl] = y.astype(ml_ref.dtype)


def _ctx_mix(layer, q, k, v, mq, mk, mv, og, dq, dk, dv, gt, gT, ml_gain, diff_gain, diff_lambda, states=None):
    lam_init = 0.8 - 0.6 * math.exp(-0.3 * layer)
    row = lambda width: pl.BlockSpec((SEQ, width), lambda b: (b, 0))
    cache = lambda width: pl.BlockSpec((None, None, SEQ, width), lambda b: (b, layer, 0, 0))
    full = lambda a: pl.BlockSpec(a.shape, lambda b: (0,) * a.ndim)
    nh = 2 * MLSTM_HEADS
    in_specs = [row(ATT_W), cache(KV_W), cache(KV_W), row(ML_W), row(ML_W), row(ML_W), row(ML_W),
                row(DF_W), cache(DF_W), cache(DF_W), row(LANES),
                pl.BlockSpec((SEQ // CHUNK, LANES, CHUNK), lambda b: (b, 0, 0)),
                full(ml_gain), full(diff_gain), full(diff_lambda)]
    args = [q, k, v, mq, mk, mv, og, dq, dk, dv, gt, gT, ml_gain, diff_gain, diff_lambda]
    aliases = {}
    if states is not None:
        aliases = {len(args) + j: 3 + j for j in range(3)}
        in_specs += [pl.BlockSpec(memory_space=pl.ANY)] * 3
        args += list(states)
    return pl.pallas_call(
        functools.partial(_ctx_mix_kernel, lam_init),
        grid=(BATCH,),
        in_specs=in_specs,
        out_specs=[row(ATT_W), row(ML_W), row(DF_W),
                   pl.BlockSpec((None, None, 2, MLSTM_HEADS, HEAD_DIM, HEAD_DIM), lambda b: (b, layer, 0, 0, 0, 0)),
                   pl.BlockSpec((None, None, nh, HEAD_DIM), lambda b: (b, layer, 0, 0)),
                   pl.BlockSpec((None, None, nh, LANES), lambda b: (b, layer, 0, 0))],
        out_shape=[jax.ShapeDtypeStruct((T_CTX, ATT_W), BF16),
                   jax.ShapeDtypeStruct((T_CTX, ML_W), BF16),
                   jax.ShapeDtypeStruct((T_CTX, DF_W), BF16),
                   jax.ShapeDtypeStruct((BATCH, DEPTH, 2, MLSTM_HEADS, HEAD_DIM, HEAD_DIM), F32),
                   jax.ShapeDtypeStruct((BATCH, DEPTH, nh, HEAD_DIM), F32),
                   jax.ShapeDtypeStruct((BATCH, DEPTH, nh, LANES), F32)],
        input_output_aliases=aliases,
        compiler_params=_params(1),
        name="ctx_mix",
    )(*args)


GQA_TQ = 512
DIFF_TQ = 1024


def _lat_gqa_kernel(q_ref, k_ref, v_ref, o_ref):
    blocks = [(slice(j * ATT_SUB, (j + 1) * ATT_SUB), slice(r * HEAD_DIM, (r + 1) * HEAD_DIM))
              for r in range(ATT_HEADS // ATT_KV_HEADS) for j in range(q_ref.shape[0] // ATT_SUB)]

    def put(i, o):
        o_ref[blocks[i]] = o.astype(o_ref.dtype)

    _attend_many(lambda i: q_ref[blocks[i]], len(blocks), k_ref[...], _with_ones(v_ref[...]), put)


def _lat_gqa(q, k_full, v_full):
    rep = ATT_HEADS // ATT_KV_HEADS
    nq = DEC_SEQ // GQA_TQ
    qspec = pl.BlockSpec((GQA_TQ, rep * HEAD_DIM), lambda b, g, i: (b * nq + i, g))
    kvspec = pl.BlockSpec((None, KEYS_LAT, HEAD_DIM), lambda b, g, i: (b, 0, g))
    return pl.pallas_call(
        _lat_gqa_kernel,
        grid=(DEC_BATCH, ATT_KV_HEADS, nq),
        in_specs=[qspec, kvspec, kvspec],
        out_specs=qspec,
        out_shape=jax.ShapeDtypeStruct((T_LAT, ATT_W), BF16),
        compiler_params=_params(3),
        name="lat_gqa",
    )(q, k_full, v_full)


def _lat_diff_kernel(lam_init, q_ref, k_ref, v_ref, dfg_ref, lam_ref, o_ref):
    lam = _diff_lambda(lam_ref, lam_init)
    rows = lambda j: slice(j * ATT_SUB, (j + 1) * ATT_SUB)

    def put(j, d):
        o_ref[rows(j), :] = d.astype(o_ref.dtype)

    _diff_attend_many(lambda j: q_ref[rows(j), :], q_ref.shape[0] // ATT_SUB, k_ref[...], _with_ones(v_ref[...]),
                      lam, dfg_ref[...], lam_init, put)


def _lat_diff(layer, dq, dk_full, dv_full, diff_gain, diff_lambda):
    lam_init = 0.8 - 0.6 * math.exp(-0.3 * layer)
    nq = DEC_SEQ // DIFF_TQ
    qspec = pl.BlockSpec((DIFF_TQ, HEAD_DIM), lambda b, h, i: (b * nq + i, h))
    kvspec = pl.BlockSpec((None, KEYS_LAT, HEAD_DIM), lambda b, h, i: (b, 0, h))
    full = lambda a: pl.BlockSpec(a.shape, lambda b, h, i: (0,) * a.ndim)
    return pl.pallas_call(
        functools.partial(_lat_diff_kernel, lam_init),
        grid=(DEC_BATCH, DIFF_HEADS, nq),
        in_specs=[qspec, kvspec, kvspec, full(diff_gain), full(diff_lambda)],
        out_specs=qspec,
        out_shape=jax.ShapeDtypeStruct((T_LAT, DF_W), BF16),
        compiler_params=_params(3),
        name="lat_diff",
    )(dq, dk_full, dv_full, diff_gain, diff_lambda)


def _lat_mlstm_kernel(q_ref, k_ref, v_ref, og_ref, gt_ref, gT_ref, C0_ref, n0_ref, m0_ref, mlg_ref,
                      o_ref, hs_ref):
    n_chunks = DEC_SEQ // CHUNK
    masks, tris = _scan_masks()

    def step(second_half, t, carry):
        items, where = [], []
        for d in range(2):
            c = t if d == 0 else n_chunks - 1 - t
            rows = pl.ds(pl.multiple_of(c * CHUNK, CHUNK), CHUNK)
            g = gt_ref[rows, :]
            gT = gT_ref[c]
            b, bT = _gate_cumsums(_split3(g), _split3(gT), tris, d)
            for h in range(MLSTM_HEADS):
                sl = slice(h * HEAD_DIM, (h + 1) * HEAD_DIM)
                ci, cf = _gate_cols(d, h)
                CN, m = carry[d * MLSTM_HEADS + h]
                items.append((q_ref[rows, sl], k_ref[rows, sl], _with_ones(v_ref[rows, sl]),
                              b[:, cf:cf + 1], bT[cf:cf + 1, :], g[:, ci:ci + 1], gT[ci:ci + 1, :], d, CN, m))
                where.append((rows, sl, h))
        res = _mlstm_chunks(items, masks)
        for (rows, sl, h), (hc, _, _) in zip(where, res):
            if second_half:
                y = _rms(hs_ref[rows, sl] + hc, mlg_ref[h:h + 1, :]) * og_ref[rows, sl].astype(F32)
                o_ref[rows, sl] = y.astype(o_ref.dtype)
            else:
                hs_ref[rows, sl] = hc
        return tuple((CN, m) for _, CN, m in res)

    init = tuple((_state_in(C0_ref[d, h], n0_ref[d, h]), m0_ref[d, h])
                 for d in range(2) for h in range(MLSTM_HEADS))
    half = n_chunks // 2
    carry = lax.fori_loop(0, half, functools.partial(step, False), init)
    lax.fori_loop(half, n_chunks, functools.partial(step, True), carry)


def _lat_mlstm(layer, mq, mk, mv, og, gt, gT, state_C, state_n6, state_m6, ml_gain):
    once = pl.Buffered(1)
    hspec = pl.BlockSpec((DEC_SEQ, ML_W), lambda b: (b, 0), pipeline_mode=once)
    st = lambda last2: pl.BlockSpec((None, None, 2, MLSTM_HEADS) + last2, lambda b: (b, layer, 0, 0, 0, 0))
    return pl.pallas_call(
        _lat_mlstm_kernel,
        grid=(DEC_BATCH,),
        in_specs=[hspec, hspec, hspec, hspec,
                  pl.BlockSpec((DEC_SEQ, LANES), lambda b: (b, 0), pipeline_mode=once),
                  pl.BlockSpec((DEC_SEQ // CHUNK, LANES, CHUNK), lambda b: (b, 0, 0), pipeline_mode=once),
                  st((HEAD_DIM, HEAD_DIM)), st((1, HEAD_DIM)), st((1, LANES)),
                  pl.BlockSpec(ml_gain.shape, lambda b: (0, 0))],
        out_specs=pl.BlockSpec((DEC_SEQ, ML_W), lambda b: (b, 0)),
        out_shape=jax.ShapeDtypeStruct((T_LAT, ML_W), BF16),
        scratch_shapes=[pltpu.VMEM((DEC_SEQ, ML_W), F32)],
        compiler_params=_params(1),
        name="lat_mlstm",
    )(mq, mk, mv, og, gt, gT, state_C, state_n6, state_m6, ml_gain)


OUT_TM = 512
FFN_TM = 512
FFN_TF = 1024
ROW_BLOCK = 16


def _cond_index(latent, tm):
    per_seq = DEC_SEQ // tm
    if latent:
        return lambda i: 1 + i // per_seq
    return lambda i: 0


def _out_proj_kernel(att_ref, ml_ref, dif_ref, w_ref, x_ref, mod_ref, g_ref, o_ref):
    mix = (_dot(att_ref[...], w_ref[0:ATT_W, :])
           + _dot(ml_ref[...], w_ref[ATT_W:ATT_W + ML_W, :])
           + _dot(dif_ref[...], w_ref[ATT_W + ML_W:, :]))
    o_ref[...] = x_ref[...] + mod_ref[2:3, :] * _rms(mix, g_ref[1:2, :])


def _out_proj(layer, att, ml, dif, w_out_b, x, mod_l, gains, latent):
    t = x.shape[0]
    tm = OUT_TM
    cond_of = _cond_index(latent, tm)
    row = lambda width: pl.BlockSpec((tm, width), lambda i: (i, 0))
    return pl.pallas_call(
        _out_proj_kernel,
        grid=(t // tm,),
        in_specs=[row(ATT_W), row(ML_W), row(DF_W),
                  pl.BlockSpec((None, D_MODEL, D_MODEL), lambda i: (layer, 0, 0), pipeline_mode=pl.Buffered(1)),
                  row(D_MODEL),
                  pl.BlockSpec((None, 6, D_MODEL), lambda i: (cond_of(i), 0, 0)),
                  pl.BlockSpec((4, D_MODEL), lambda i: (0, 0))],
        out_specs=row(D_MODEL),
        out_shape=jax.ShapeDtypeStruct((t, D_MODEL), F32),
        compiler_params=_params(1),
        name="out_proj",
    )(att, ml, dif, w_out_b, x, mod_l, gains)


def _ffn_kernel(x_ref, mod_ref, g_ref, w1_ref, w2_ref, o_ref, hn_ref):
    f = pl.program_id(1)

    def row_blocks(body):
        for i in range(x_ref.shape[0] // ROW_BLOCK):
            body(slice(i * ROW_BLOCK, (i + 1) * ROW_BLOCK))

    @pl.when(f == 0)
    def _():
        def body(rows):
            hn = _rms(x_ref[rows, :], g_ref[2:3, :]) * (1.0 + mod_ref[4:5, :]) + mod_ref[3:4, :]
            hn_ref[rows, :] = hn.astype(hn_ref.dtype)
            o_ref[rows, :] = jnp.zeros((ROW_BLOCK, o_ref.shape[1]), o_ref.dtype)
        row_blocks(body)

    h = jnp.maximum(_dot(hn_ref[...], w1_ref[...]), 0.0)
    o_ref[...] += _dot((h * h).astype(BF16), w2_ref[...])

    @pl.when(f == pl.num_programs(1) - 1)
    def _():
        def body(rows):
            o_ref[rows, :] = x_ref[rows, :] + mod_ref[5:6, :] * _rms(o_ref[rows, :], g_ref[3:4, :])
        row_blocks(body)


def _ffn(layer, x, mod_l, gains, w1_b, w2_b, latent):
    t = x.shape[0]
    tm, tf = FFN_TM, FFN_TF
    cond_of = _cond_index(latent, tm)
    return pl.pallas_call(
        _ffn_kernel,
        grid=(t // tm, D_FF // tf),
        in_specs=[pl.BlockSpec((tm, D_MODEL), lambda i, f: (i, 0)),
                  pl.BlockSpec((None, 6, D_MODEL), lambda i, f: (cond_of(i), 0, 0)),
                  pl.BlockSpec((4, D_MODEL), lambda i, f: (0, 0)),
                  pl.BlockSpec((None, D_MODEL, tf), lambda i, f: (layer, 0, f)),
                  pl.BlockSpec((None, tf, D_MODEL), lambda i, f: (layer, f, 0))],
        out_specs=pl.BlockSpec((tm, D_MODEL), lambda i, f: (i, 0)),
        out_shape=jax.ShapeDtypeStruct((t, D_MODEL), F32),
        scratch_shapes=[pltpu.VMEM((tm, D_MODEL), BF16)],
        compiler_params=_params(2),
        name="ffn",
    )(x, mod_l, gains, w1_b, w2_b)


def _rope_tables():
    def angles(dim):
        rows = DEC_SEQ // GRID_W
        row_idx = jnp.repeat(jnp.arange(rows), GRID_W).astype(F32)
        col_idx = jnp.tile(jnp.arange(GRID_W), rows).astype(F32)
        n_freq = dim // 4
        inv = ROPE_THETA ** (-jnp.arange(n_freq, dtype=F32) / n_freq)
        ang = jnp.concatenate([row_idx[:, None] * inv, col_idx[:, None] * inv], axis=-1)
        return jnp.cos(ang), jnp.sin(ang)

    c, s = angles(HEAD_DIM)
    c2, s2 = angles(DIFF_QK_DIM)
    c128 = jnp.concatenate([c, c], axis=-1)
    s128 = jnp.concatenate([-s, s], axis=-1)
    c64 = jnp.concatenate([c2, c2, c2, c2], axis=-1)
    s64 = jnp.concatenate([-s2, s2, -s2, s2], axis=-1)
    return c128, s128, c64, s64


def kernel(x_prompt, x_sample, c, cache_gqa_k, cache_gqa_v, cache_diff_k, cache_diff_v, state_mlstm_C,
           state_mlstm_n, state_mlstm_m, c_ctx, w_ada, b_ada, norm_gain, w_in, w_out, qk_gain,
           mlstm_gate_bias, mlstm_head_gain, diff_lambda, diff_head_gain, w_ff1, w_ff2):
    cond3 = jnp.concatenate([c_ctx[None, :], c], axis=0)
    mod = _modulation(cond3, w_ada, b_ada)

    w_in_t = jnp.swapaxes(w_in, 1, 2).astype(BF16)
    w_out_b = w_out.astype(BF16)
    w1_b = w_ff1.astype(BF16)
    w2_b = w_ff2.astype(BF16)
    gate_bias = jnp.pad(mlstm_gate_bias.reshape(DEPTH, 1, N_GATES), ((0, 0), (0, 0), (0, LANES - N_GATES)))
    diff_gain = diff_head_gain.reshape(DEPTH, 1, HEAD_DIM)
    rope = _rope_tables()

    state_n6 = state_mlstm_n.reshape(DEC_BATCH, DEPTH, 2, MLSTM_HEADS, 1, HEAD_DIM)
    state_m6 = jnp.broadcast_to(state_mlstm_m[..., None, None], (DEC_BATCH, DEPTH, 2, MLSTM_HEADS, 1, LANES))

    xp = x_prompt.reshape(T_CTX, D_MODEL)
    xs = x_sample.reshape(T_LAT, D_MODEL)
    caches = None
    states = None

    for l in range(DEPTH):
        (q, k, v, mq, mk, mv, og, dq, dk, dv, gt, gT) = _in_proj(
            l, xp, mod[l], norm_gain[l], w_in_t, qk_gain[l], gate_bias[l], None, False, caches)
        caches = (k, v, dk, dv)
        att, ml, dif, *states = _ctx_mix(l, q, k, v, mq, mk, mv, og, dq, dk, dv, gt, gT,
                                         mlstm_head_gain[l], diff_gain[l], diff_lambda[l], states)
        xp = _out_proj(l, att, ml, dif, w_out_b, xp, mod[l], norm_gain[l], False)
        xp = _ffn(l, xp, mod[l], norm_gain[l], w1_b, w2_b, False)

        (q, k, v, mq, mk, mv, og, dq, dk, dv, gt, gT) = _in_proj(
            l, xs, mod[l], norm_gain[l], w_in_t, qk_gain[l], gate_bias[l], rope, True)
        as_keys = lambda new_rows, cache, width: jnp.concatenate(
            [new_rows.reshape(DEC_BATCH, DEC_SEQ, width),
             cache[:, l].reshape(DEC_BATCH, PAST_LEN, width).astype(BF16)], axis=1)
        att = _lat_gqa(q, as_keys(k, cache_gqa_k, KV_W), as_keys(v, cache_gqa_v, KV_W))
        dif = _lat_diff(l, dq, as_keys(dk, cache_diff_k, DF_W), as_keys(dv, cache_diff_v, DF_W),
                        diff_gain[l], diff_lambda[l])
        ml = _lat_mlstm(l, mq, mk, mv, og, gt, gT, state_mlstm_C, state_n6, state_m6, mlstm_head_gain[l])
        xs = _out_proj(l, att, ml, dif, w_out_b, xs, mod[l], norm_gain[l], True)
        xs = _ffn(l, xs, mod[l], norm_gain[l], w1_b, w2_b, True)

    new_k, new_v, new_dk, new_dv = caches
    new_C, new_n, new_m = states
    return (xp.reshape(BATCH, SEQ, D_MODEL), xs.reshape(DEC_BATCH, DEC_SEQ, D_MODEL),
            new_k.reshape(BATCH, DEPTH, SEQ, ATT_KV_HEADS, HEAD_DIM),
            new_v.reshape(BATCH, DEPTH, SEQ, ATT_KV_HEADS, HEAD_DIM),
            new_dk.reshape(BATCH, DEPTH, SEQ, DIFF_HEADS, HEAD_DIM),
            new_dv.reshape(BATCH, DEPTH, SEQ, DIFF_HEADS, HEAD_DIM),
            new_C,
            new_n.reshape(BATCH, DEPTH, 2, MLSTM_HEADS, HEAD_DIM),
            new_m[..., 0].reshape(BATCH, DEPTH, 2, MLSTM_HEADS))
```
